```python
import math
import jax, jax.numpy as jnp
from jax import lax
import numpy as np

D_MODEL = 1024
BATCH = 32
SEQ = 2048
DEPTH = 4

N_MIXERS = 4
N_S5 = (DEPTH + 3) // 4
N_GLA = (DEPTH + 2) // 4
N_DIFF = (DEPTH + 1) // 4
N_SSD = DEPTH // 4

EXPAND = 2
D_INNER = EXPAND * D_MODEL
EPS = 1e-6

MEM_LEN = 256
X_HEADS = 4
X_HEAD_DIM = 128
D_X = X_HEADS * X_HEAD_DIM
D_GATE = D_INNER + D_X

S5_GROUP = 16
S5_GROUPS = D_INNER // S5_GROUP
S5_STATE = 64
S5_CHUNK = 64

GLA_HEADS = 4
GLA_DK = D_MODEL // 2 // GLA_HEADS
GLA_DV = D_INNER // GLA_HEADS
GLA_RANK = 16
GLA_TAU = 16.0
GLA_CHUNK = 32

DIFF_HEADS = 16
DIFF_HALF = D_INNER // DIFF_HEADS // 2
DIFF_VDIM = 2 * DIFF_HALF
Q_BLOCK = 128
ROPE_THETA = 10000.0
MAX_POS_OFFSET = 4096

SSD_HEAD_DIM = 64
SSD_HEADS = D_INNER // SSD_HEAD_DIM
SSD_GROUPS = 8
SSD_HPG = SSD_HEADS // SSD_GROUPS
SSD_STATE = 128
SSD_CONV = 4
SSD_CONV_CH = D_INNER + 2 * SSD_GROUPS * SSD_STATE
SSD_CHUNK = 64

S5_IN = D_INNER + D_GATE + D_X
GLA_IN = 2 * GLA_HEADS * GLA_DK + D_INNER + GLA_RANK + D_GATE + D_X
DIFF_IN = 3 * D_INNER + D_GATE + D_X
SSD_IN = SSD_CONV_CH + SSD_HEADS + D_GATE + D_X

kernel_name = "hybrid_s5_gla_diff_ssd_memory"

F32 = jnp.float32


def rms_norm(x, g):
    xf = x.astype(F32)
    y = xf * lax.rsqrt(jnp.mean(xf * xf, axis=-1, keepdims=True) + EPS)
    return (y * g.astype(F32)).astype(x.dtype)


def split_cols(t, sizes):
    idx = np.cumsum(np.array(sizes))[:-1].tolist()
    return jnp.split(t, idx, axis=-1)


def rope_tables(positions, dim):
    inv = ROPE_THETA ** (-jnp.arange(0, dim, 2, dtype=F32) / dim)
    ang = positions.astype(F32)[..., None] * inv
    return jnp.cos(ang)[:, :, None, None, :], jnp.sin(ang)[:, :, None, None, :]


def rope(x, cos, sin):
    half = x.shape[-1] // 2
    x1, x2 = x[..., :half], x[..., half:]
    return jnp.concatenate([x1 * cos - x2 * sin, x2 * cos + x1 * sin], axis=-1).astype(x.dtype)


def _complex_affine_combine(e1, e2):
    a1r, a1i, b1r, b1i = e1
    a2r, a2i, b2r, b2i = e2
    ar = a2r * a1r - a2i * a1i
    ai = a2r * a1i + a2i * a1r
    br = a2r * b1r - a2i * b1i + b2r
    bi = a2r * b1i + a2i * b1r + b2i
    return ar, ai, br, bi


def s5_mixer(u, z, lam_re, lam_im, log_step, b_re, b_im, c_re, c_im, d_skip, w_glu):
    bsz, seq, _ = u.shape
    nc = seq // S5_CHUNK
    lam_re = lam_re.astype(F32)
    lam_im = lam_im.astype(F32)
    step = jnp.exp(log_step.astype(F32))[:, None]
    mag = jnp.exp(lam_re * step)
    lb_re = mag * jnp.cos(lam_im * step)
    lb_im = mag * jnp.sin(lam_im * step)
    den = lam_re * lam_re + lam_im * lam_im
    nr = lb_re - 1.0
    co_re = (nr * lam_re + lb_im * lam_im) / den
    co_im = (lb_im * lam_re - nr * lam_im) / den
    b_re = b_re.astype(F32)
    b_im = b_im.astype(F32)
    bb_re = co_re[..., None] * b_re - co_im[..., None] * b_im
    bb_im = co_re[..., None] * b_im + co_im[..., None] * b_re
    c_re = c_re.astype(F32)
    c_im = c_im.astype(F32)
    a_re = jnp.broadcast_to(lb_re, (S5_CHUNK, 1) + lb_re.shape)
    a_im = jnp.broadcast_to(lb_im, (S5_CHUNK, 1) + lb_im.shape)
    ug = u.astype(F32).reshape(bsz, nc, S5_CHUNK, S5_GROUPS, S5_GROUP).transpose(1, 2, 0, 3, 4)

    def chunk_step(carry, u_c):
        h_re, h_im = carry
        dr = jnp.einsum('tbgh,gph->tbgp', u_c, bb_re)
        di = jnp.einsum('tbgh,gph->tbgp', u_c, bb_im)
        pr, pim, sr, si = lax.associative_scan(_complex_affine_combine, (a_re, a_im, dr, di), axis=0)
        sr = sr + pr * h_re - pim * h_im
        si = si + pr * h_im + pim * h_re
        y = jnp.einsum('tbgp,ghp->tbgh', sr, c_re) - jnp.einsum('tbgp,ghp->tbgh', si, c_im)
        return (sr[-1], si[-1]), y

    h0 = jnp.zeros((bsz, S5_GROUPS, S5_STATE), F32)
    _, ys = lax.scan(chunk_step, (h0, h0), ug)
    y = ys.transpose(2, 0, 1, 3, 4).reshape(bsz, seq, D_INNER) + d_skip.astype(F32) * u.astype(F32)
    g = jax.nn.gelu(y)
    out = g * jax.nn.sigmoid(g @ w_glu.astype(F32))
    return (out * jax.nn.silu(z.astype(F32))).astype(u.dtype)


def gla_mixer(q, k, v, gk_low, z, w_gk2, b_gk2, norm_g):
    bsz, seq, _ = v.shape
    nc = seq // GLA_CHUNK
    log_a = jax.nn.log_sigmoid(gk_low.astype(F32) @ w_gk2.astype(F32) + b_gk2.astype(F32)) / GLA_TAU

    def heads(t, d):
        return t.astype(F32).reshape(bsz, nc, GLA_CHUNK, GLA_HEADS, d).transpose(1, 0, 3, 2, 4)

    qc = heads(q, GLA_DK) * GLA_DK ** -0.5
    kc = heads(k, GLA_DK)
    vc = heads(v, GLA_DV)
    gc = heads(log_a, GLA_DK)
    tri = jnp.tril(jnp.ones((GLA_CHUNK, GLA_CHUNK), dtype=bool))

    def step(state, inp):
        qi, ki, vi, gi = inp
        b = jnp.cumsum(gi, axis=2)
        o_inter = jnp.einsum('bhtk,bhkv->bhtv', qi * jnp.exp(b), state)
        rel = jnp.where(tri[:, :, None], b[:, :, :, None, :] - b[:, :, None, :, :], -jnp.inf)
        att = jnp.einsum('bhtk,bhsk,bhtsk->bhts', qi, ki, jnp.exp(rel))
        o_intra = jnp.einsum('bhts,bhsv->bhtv', att, vi)
        b_last = b[:, :, -1]
        new_state = state * jnp.exp(b_last)[..., None] + jnp.einsum(
            'bhsk,bhsv->bhkv', ki * jnp.exp(b_last[:, :, None] - b), vi)
        return new_state, o_inter + o_intra

    s0 = jnp.zeros((bsz, GLA_HEADS, GLA_DK, GLA_DV), F32)
    _, o = lax.scan(step, s0, (qc, kc, vc, gc))
    o = o.transpose(1, 0, 3, 2, 4).reshape(bsz, seq, GLA_HEADS, GLA_DV)
    o = rms_norm(o, norm_g).reshape(bsz, seq, D_INNER)
    return (o * jax.nn.silu(z.astype(F32))).astype(v.dtype)


def diff_attention(q, k, v, z, cos, sin, q_g, k_g, lq1, lk1, lq2, lk2, subln_g, lam_init):
    bsz, seq, _ = q.shape
    shp = (bsz, seq, DIFF_HEADS, 2, DIFF_HALF)
    q = rope(rms_norm(q.reshape(shp), q_g), cos, sin) * (DIFF_HALF ** -0.5)
    k = rope(rms_norm(k.reshape(shp), k_g), cos, sin)
    v = v.reshape(bsz, seq, DIFF_HEADS, DIFF_VDIM)
    lam = (jnp.exp(jnp.sum(lq1.astype(F32) * lk1.astype(F32)))
           - jnp.exp(jnp.sum(lq2.astype(F32) * lk2.astype(F32))) + lam_init)
    nb = seq // Q_BLOCK
    qb = q.reshape((bsz, nb, Q_BLOCK) + shp[2:]).swapaxes(0, 1)
    kpos = jnp.arange(seq)

    def block(args):
        qi, bi = args
        s = jnp.einsum('bqhcd,bkhcd->bhcqk', qi, k).astype(F32)
        qpos = bi * Q_BLOCK + jnp.arange(Q_BLOCK)
        s = jnp.where(kpos[None, :] <= qpos[:, None], s, -jnp.inf)
        p = jax.nn.softmax(s, axis=-1)
        w = p[:, :, 0] - lam * p[:, :, 1]
        return jnp.einsum('bhqk,bkhv->bqhv', w.astype(v.dtype), v)

    o = lax.map(block, (qb, jnp.arange(nb))).swapaxes(0, 1).reshape(bsz, seq, DIFF_HEADS, DIFF_VDIM)
    o = rms_norm(o, subln_g).astype(F32) * (1.0 - lam_init)
    return (o.reshape(bsz, seq, D_INNER) * jax.nn.silu(z.astype(F32))).astype(z.dtype)


def causal_depthwise_conv(x, w, b):
    y = lax.conv_general_dilated(x, w[:, None, :], window_strides=(1,), padding=[(SSD_CONV - 1, 0)],
                                 dimension_numbers=('NWC', 'WIO', 'NWC'), feature_group_count=x.shape[-1])
    return y + b


def ssd_mixer(xbc, dt_raw, z, conv_w, conv_b, dt_bias, a_log, d_skip, norm_g):
    bsz, seq, _ = xbc.shape
    nc = seq // SSD_CHUNK
    xbc = jax.nn.silu(causal_depthwise_conv(xbc.astype(F32), conv_w.astype(F32), conv_b.astype(F32)))
    xs, bm, cm = split_cols(xbc, (D_INNER, SSD_GROUPS * SSD_STATE, SSD_GROUPS * SSD_STATE))
    dt = jax.nn.softplus(dt_raw.astype(F32) + dt_bias.astype(F32))
    a = -jnp.exp(a_log.astype(F32)).reshape(SSD_GROUPS, SSD_HPG)

    def chunks(t, tail):
        return t.reshape((bsz, nc, SSD_CHUNK) + tail).swapaxes(0, 1)

    xc = chunks(xs, (SSD_GROUPS, SSD_HPG, SSD_HEAD_DIM))
    dtc = chunks(dt, (SSD_GROUPS, SSD_HPG))
    bc = chunks(bm, (SSD_GROUPS, SSD_STATE))
    cc = chunks(cm, (SSD_GROUPS, SSD_STATE))
    tri = jnp.tril(jnp.ones((SSD_CHUNK, SSD_CHUNK), dtype=bool))

    def step(state, inp):
        xi, dti, bi, ci = inp
        cum = jnp.cumsum(dti * a, axis=1)
        seg = jnp.where(tri[None, :, :, None, None], cum[:, :, None] - cum[:, None, :], -jnp.inf)
        cb = jnp.einsum('btgn,bsgn->btsg', ci, bi)
        y_intra = jnp.einsum('btsg,btsgr,bsgrp->btgrp', cb, jnp.exp(seg), xi * dti[..., None])
        y_inter = jnp.einsum('btgn,bgrpn->btgrp', ci, state) * jnp.exp(cum)[..., None]
        last = cum[:, -1]
        w_s = jnp.exp(last[:, None] - cum) * dti
        new_state = state * jnp.exp(last)[..., None, None] + jnp.einsum('bsgn,bsgr,bsgrp->bgrpn', bi, w_s, xi)
        return new_state, y_intra + y_inter

    s0 = jnp.zeros((bsz, SSD_GROUPS, SSD_HPG, SSD_HEAD_DIM, SSD_STATE), F32)
    _, ys = lax.scan(step, s0, (xc, dtc, bc, cc))
    y = ys.swapaxes(0, 1).reshape(bsz, seq, SSD_HEADS, SSD_HEAD_DIM)
    y = y + d_skip.astype(F32)[:, None] * xs.reshape(bsz, seq, SSD_HEADS, SSD_HEAD_DIM)
    y = y.reshape(bsz, seq, D_INNER) * jax.nn.silu(z.astype(F32))
    return rms_norm(y, norm_g).astype(z.dtype)


def memory_cross_attention(qx, mem_n, w_kv, q_g, k_g):
    bsz, seq, _ = qx.shape
    k, v = jnp.split(mem_n @ w_kv, 2, axis=-1)
    q = rms_norm(qx.reshape(bsz, seq, X_HEADS, X_HEAD_DIM), q_g)
    k = rms_norm(k.reshape(bsz, -1, X_HEADS, X_HEAD_DIM), k_g)
    v = v.reshape(bsz, -1, X_HEADS, X_HEAD_DIM)
    s = jnp.einsum('bshd,bmhd->bhsm', q, k).astype(F32) * (X_HEAD_DIM ** -0.5)
    p = jax.nn.softmax(s, axis=-1).astype(v.dtype)
    return jnp.einsum('bhsm,bmhd->bshd', p, v).reshape(bsz, seq, D_X)


def setup_inputs(seed: int = 0) -> dict:
    key = jax.random.key(seed)
    ks = iter(jax.random.split(key, 64))

    def nrm(shape, scale):
        return jax.random.normal(next(ks), shape, F32) * scale

    def gain(shape):
        return 1.0 + nrm(shape, 0.02)

    def unif(shape, lo, hi):
        return jax.random.uniform(next(ks), shape, F32, lo, hi)

    x = nrm((BATCH, SEQ, D_MODEL), 1.0)
    mem = nrm((BATCH, MEM_LEN, D_MODEL), 1.0)
    offs = jax.random.randint(next(ks), (BATCH, 1), 0, MAX_POS_OFFSET, dtype=jnp.int32)
    positions = (offs + jnp.arange(SEQ, dtype=jnp.int32)[None, :]).astype(jnp.int32)
    din = D_MODEL ** -0.5

    norm_g = gain((DEPTH, D_MODEL))
    w_out = nrm((DEPTH, D_GATE, D_MODEL), D_GATE ** -0.5)
    mem_norm_g = gain((D_MODEL,))
    w_mem_kv = nrm((DEPTH, D_MODEL, 2 * D_X), din)
    xq_g = gain((DEPTH, X_HEAD_DIM))
    xk_g = gain((DEPTH, X_HEAD_DIM))

    s5_w_in = nrm((N_S5, D_MODEL, S5_IN), din)
    s5_lam_re = -0.5 + nrm((N_S5, S5_GROUPS, S5_STATE), 0.01)
    s5_lam_im = math.pi * jnp.arange(S5_STATE, dtype=F32) + nrm((N_S5, S5_GROUPS, S5_STATE), 0.01)
    s5_log_step = unif((N_S5, S5_GROUPS), math.log(1e-3), math.log(1e-1))
    s5_b_re = nrm((N_S5, S5_GROUPS, S5_STATE, S5_GROUP), (2 * S5_GROUP) ** -0.5)
    s5_b_im = nrm((N_S5, S5_GROUPS, S5_STATE, S5_GROUP), (2 * S5_GROUP) ** -0.5)
    s5_c_re = nrm((N_S5, S5_GROUPS, S5_GROUP, S5_STATE), (2 * S5_STATE) ** -0.5)
    s5_c_im = nrm((N_S5, S5_GROUPS, S5_GROUP, S5_STATE), (2 * S5_STATE) ** -0.5)
    s5_d = nrm((N_S5, D_INNER), 1.0)
    s5_w_glu = nrm((N_S5, D_INNER, D_INNER), D_INNER ** -0.5)

    gla_w_in = nrm((N_GLA, D_MODEL, GLA_IN), din)
    gla_w_gk2 = nrm((N_GLA, GLA_RANK, GLA_HEADS * GLA_DK), GLA_RANK ** -0.5)
    gla_b_gk2 = nrm((N_GLA, GLA_HEADS * GLA_DK), 0.1)
    gla_norm_g = gain((N_GLA, GLA_DV))

    diff_w_in = nrm((N_DIFF, D_MODEL, DIFF_IN), din)
    diff_q_g = gain((N_DIFF, DIFF_HALF))
    diff_k_g = gain((N_DIFF, DIFF_HALF))
    diff_lq1 = nrm((N_DIFF, DIFF_HALF), 0.1)
    diff_lk1 = nrm((N_DIFF, DIFF_HALF), 0.1)
    diff_lq2 = nrm((N_DIFF, DIFF_HALF), 0.1)
    diff_lk2 = nrm((N_DIFF, DIFF_HALF), 0.1)
    diff_subln_g = gain((N_DIFF, DIFF_VDIM))

    ssd_w_in = nrm((N_SSD, D_MODEL, SSD_IN), din)
    ssd_conv_w = nrm((N_SSD, SSD_CONV, SSD_CONV_CH), SSD_CONV ** -0.5)
    ssd_conv_b = nrm((N_SSD, SSD_CONV_CH), 0.02)
    dt0 = jnp.exp(unif((N_SSD, SSD_HEADS), math.log(1e-3), math.log(1e-1)))
    ssd_dt_bias = dt0 + jnp.log(-jnp.expm1(-dt0))
    ssd_a_log = jnp.log(unif((N_SSD, SSD_HEADS), 1.0, 16.0))
    ssd_d = 1.0 + nrm((N_SSD, SSD_HEADS), 0.1)
    ssd_norm_g = gain((N_SSD, D_INNER))

    return {
        'x': x, 'mem': mem, 'positions': positions,
        'norm_g': norm_g, 'w_out': w_out, 'mem_norm_g': mem_norm_g, 'w_mem_kv': w_mem_kv,
        'xq_g': xq_g, 'xk_g': xk_g,
        's5_w_in': s5_w_in, 's5_lam_re': s5_lam_re, 's5_lam_im': s5_lam_im, 's5_log_step': s5_log_step,
        's5_b_re': s5_b_re, 's5_b_im': s5_b_im, 's5_c_re': s5_c_re, 's5_c_im': s5_c_im,
        's5_d': s5_d, 's5_w_glu': s5_w_glu,
        'gla_w_in': gla_w_in, 'gla_w_gk2': gla_w_gk2, 'gla_b_gk2': gla_b_gk2, 'gla_norm_g': gla_norm_g,
        'diff_w_in': diff_w_in, 'diff_q_g': diff_q_g, 'diff_k_g': diff_k_g,
        'diff_lq1': diff_lq1, 'diff_lk1': diff_lk1, 'diff_lq2': diff_lq2, 'diff_lk2': diff_lk2,
        'diff_subln_g': diff_subln_g,
        'ssd_w_in': ssd_w_in, 'ssd_conv_w': ssd_conv_w, 'ssd_conv_b': ssd_conv_b,
        'ssd_dt_bias': ssd_dt_bias, 'ssd_a_log': ssd_a_log, 'ssd_d': ssd_d, 'ssd_norm_g': ssd_norm_g,
    }


def reference(x, mem, positions, norm_g, w_out, mem_norm_g, w_mem_kv, xq_g, xk_g,
              s5_w_in, s5_lam_re, s5_lam_im, s5_log_step, s5_b_re, s5_b_im, s5_c_re, s5_c_im,
              s5_d, s5_w_glu,
              gla_w_in, gla_w_gk2, gla_b_gk2, gla_norm_g,
              diff_w_in, diff_q_g, diff_k_g, diff_lq1, diff_lk1, diff_lq2, diff_lk2, diff_subln_g,
              ssd_w_in, ssd_conv_w, ssd_conv_b, ssd_dt_bias, ssd_a_log, ssd_d, ssd_norm_g):
    mem_n = rms_norm(mem, mem_norm_g)
    cos, sin = rope_tables(positions, DIFF_HALF)
    tail = (D_INNER, D_X, D_X)
    for i in range(DEPTH):
        kind, j = i % N_MIXERS, i // N_MIXERS
        h = rms_norm(x, norm_g[i])
        if kind == 0:
            u, z_mix, z_x, qx = split_cols(h @ s5_w_in[j], (D_INNER,) + tail)
            branch = s5_mixer(u, z_mix, s5_lam_re[j], s5_lam_im[j], s5_log_step[j], s5_b_re[j], s5_b_im[j],
                              s5_c_re[j], s5_c_im[j], s5_d[j], s5_w_glu[j])
        elif kind == 1:
            q, k, v, gk, z_mix, z_x, qx = split_cols(
                h @ gla_w_in[j], (GLA_HEADS * GLA_DK, GLA_HEADS * GLA_DK, D_INNER, GLA_RANK) + tail)
            branch = gla_mixer(q, k, v, gk, z_mix, gla_w_gk2[j], gla_b_gk2[j], gla_norm_g[j])
        elif kind == 2:
            q, k, v, z_mix, z_x, qx = split_cols(h @ diff_w_in[j], (D_INNER, D_INNER, D_INNER) + tail)
            lam_init = 0.8 - 0.6 * math.exp(-0.3 * i)
            branch = diff_attention(q, k, v, z_mix, cos, sin, diff_q_g[j], diff_k_g[j], diff_lq1[j], diff_lk1[j],
                                    diff_lq2[j], diff_lk2[j], diff_subln_g[j], lam_init)
        else:
            xbc, dt_raw, z_mix, z_x, qx = split_cols(h @ ssd_w_in[j], (SSD_CONV_CH, SSD_HEADS) + tail)
            branch = ssd_mixer(xbc, dt_raw, z_mix, ssd_conv_w[j], ssd_conv_b[j], ssd_dt_bias[j],
                               ssd_a_log[j], ssd_d[j], ssd_norm_g[j])
        mem_out = memory_cross_attention(qx, mem_n, w_mem_kv[i], xq_g[i], xk_g[i]) * jax.nn.silu(z_x)
        x = x + jnp.concatenate([branch, mem_out], axis=-1) @ w_out[i]
    return x
```

```python
import functools
import math

import numpy as np
import jax
import jax.numpy as jnp
from jax import lax
from jax.experimental import pallas as pl
from jax.experimental.pallas import tpu as pltpu

F32 = jnp.float32
BF16 = jnp.bfloat16
HIGHEST = lax.Precision.HIGHEST

D_MODEL = 1024
D_INNER = 2048
EPS = 1e-6
MEM_LEN = 256
X_HEADS = 4
X_HEAD_DIM = 128
D_X = X_HEADS * X_HEAD_DIM
D_GATE = D_INNER + D_X
S5_GROUP = 16
S5_GROUPS = D_INNER // S5_GROUP
S5_STATE = 64
GLA_HEADS = 4
GLA_DK = 128
GLA_DV = 512
GLA_RANK = 16
GLA_TAU = 16.0
DIFF_HEADS = 16
DIFF_HALF = 64
DIFF_VDIM = 128
ROPE_THETA = 10000.0
SSD_HEAD_DIM = 64
SSD_HEADS = 32
SSD_GROUPS = 8
SSD_HPG = 4
SSD_STATE = 128
SSD_CONV = 4
SSD_CONV_CH = D_INNER + 2 * SSD_GROUPS * SSD_STATE

LANES = 128
VMEM_LIMIT = 48 * 1024 * 1024

PROJ_TM = 1024
PROJ_TN = 1024
OUT_TM = 512
S5_BT = 16
S5_TS = 128
S5_SLICES = D_INNER // LANES
GLA_TS = 256
GLA_CHUNK = 128
GLA_SUB = 16
DIFF_TQ = 512
DIFF_TK = 512
SSD_TS = 256
SSD_CHUNK = 128


def _cparams(sem):
    return pltpu.CompilerParams(dimension_semantics=sem, vmem_limit_bytes=VMEM_LIMIT)


def _sigmoid(x):
    return 1.0 / (1.0 + jnp.exp(-x))


def _silu(x):
    return x * _sigmoid(x)


def _in_proj_body(x_ref, g_ref, w_ref, *rest, has_aux):
    if has_aux:
        waux_ref, o_ref, oaux_ref, h_ref = rest
    else:
        o_ref, h_ref = rest

    @pl.when(pl.program_id(1) == 0)
    def _():
        x = x_ref[...]
        h = x * lax.rsqrt(jnp.mean(x * x, axis=-1, keepdims=True) + EPS) * g_ref[...]
        h_ref[...] = h.astype(BF16)
        if has_aux:
            oaux_ref[...] = jnp.dot(h, waux_ref[...], precision=HIGHEST, preferred_element_type=F32)

    o_ref[...] = jnp.dot(h_ref[...], w_ref[...], preferred_element_type=F32).astype(o_ref.dtype)


def in_proj(x2d, g, w_bf16, w_aux=None):
    n, d = x2d.shape
    nc = w_bf16.shape[1]
    tm = min(PROJ_TM, n)
    tn = PROJ_TN
    assert n % tm == 0 and nc % tn == 0
    has_aux = w_aux is not None
    in_specs = [
        pl.BlockSpec((tm, d), lambda i, j: (i, 0)),
        pl.BlockSpec((1, d), lambda i, j: (0, 0)),
        pl.BlockSpec((d, tn), lambda i, j: (0, j)),
    ]
    args = [x2d, g.reshape(1, d).astype(F32), w_bf16]
    out_shape = [jax.ShapeDtypeStruct((n, nc), BF16)]
    out_specs = [pl.BlockSpec((tm, tn), lambda i, j: (i, j))]
    if has_aux:
        in_specs.append(pl.BlockSpec((d, LANES), lambda i, j: (0, 0)))
        args.append(w_aux.astype(F32))
        out_shape.append(jax.ShapeDtypeStruct((n, LANES), F32))
        out_specs.append(pl.BlockSpec((tm, LANES), lambda i, j: (i, 0)))
    res = pl.pallas_call(
        functools.partial(_in_proj_body, has_aux=has_aux),
        grid=(n // tm, nc // tn),
        in_specs=in_specs,
        out_specs=out_specs,
        out_shape=out_shape,
        scratch_shapes=[pltpu.VMEM((tm, d), BF16)],
        compiler_params=_cparams(("parallel", "arbitrary")),
        name="in_proj_aux" if has_aux else "in_proj",
    )(*args)
    return (res[0], res[1]) if has_aux else res[0]


def _mem_kv_body(mem_ref, mg_ref, w_ref, kg_ref, k_ref, v_ref):
    m = mem_ref[0]
    mn = m * lax.rsqrt(jnp.mean(m * m, axis=-1, keepdims=True) + EPS) * mg_ref[...]
    kv = jnp.dot(mn.astype(BF16), w_ref[0], preferred_element_type=F32)
    kg = kg_ref[0]
    for h in range(X_HEADS):
        kh = kv[:, h * X_HEAD_DIM:(h + 1) * X_HEAD_DIM]
        kh = kh * lax.rsqrt(jnp.mean(kh * kh, axis=-1, keepdims=True) + EPS) * kg
        k_ref[0, 0, :, h * X_HEAD_DIM:(h + 1) * X_HEAD_DIM] = kh.astype(BF16)
    v_ref[0, 0] = kv[:, D_X:].astype(BF16)


def mem_kv(mem, mem_norm_g, w_mem_kv, xk_g):
    bsz, mlen, d = mem.shape
    depth = w_mem_kv.shape[0]
    return pl.pallas_call(
        _mem_kv_body,
        grid=(depth, bsz),
        in_specs=[
            pl.BlockSpec((1, mlen, d), lambda l, b: (b, 0, 0)),
            pl.BlockSpec((1, d), lambda l, b: (0, 0)),
            pl.BlockSpec((1, d, 2 * D_X), lambda l, b: (l, 0, 0)),
            pl.BlockSpec((1, 1, X_HEAD_DIM), lambda l, b: (l, 0, 0)),
        ],
        out_specs=[
            pl.BlockSpec((1, 1, mlen, D_X), lambda l, b: (l, b, 0, 0)),
            pl.BlockSpec((1, 1, mlen, D_X), lambda l, b: (l, b, 0, 0)),
        ],
        out_shape=[jax.ShapeDtypeStruct((depth, bsz, mlen, D_X), BF16)] * 2,
        compiler_params=_cparams(("parallel", "parallel")),
        name="mem_kv",
    )(mem, mem_norm_g.reshape(1, d).astype(F32), w_mem_kv.astype(BF16),
      xk_g.reshape(depth, 1, X_HEAD_DIM).astype(F32))


def _out_body(br_ref, zx_ref, qx_ref, k_ref, v_ref, x_ref, w_ref, qg_ref, o_ref):
    q = qx_ref[0].astype(F32)
    zx = zx_ref[0].astype(F32)
    qg = qg_ref[...]
    outs = []
    for h in range(X_HEADS):
        sl = slice(h * X_HEAD_DIM, (h + 1) * X_HEAD_DIM)
        qh = q[:, sl]
        qn = qh * lax.rsqrt(jnp.mean(qh * qh, axis=-1, keepdims=True) + EPS) * qg
        s = lax.dot_general(qn.astype(BF16), k_ref[0, :, sl], (((1,), (1,)), ((), ())),
                            preferred_element_type=F32) * (X_HEAD_DIM ** -0.5)
        p = jnp.exp(s - jnp.max(s, axis=-1, keepdims=True))
        l = jnp.sum(p, axis=-1, keepdims=True)
        oh = jnp.dot(p.astype(BF16), v_ref[0, :, sl], preferred_element_type=F32) / l
        outs.append((oh * _silu(zx[:, sl])).astype(BF16))
    mem_out = jnp.concatenate(outs, axis=-1)
    acc = jnp.dot(br_ref[0], w_ref[:D_INNER, :], preferred_element_type=F32)
    acc = acc + jnp.dot(mem_out, w_ref[D_INNER:, :], preferred_element_type=F32)
    o_ref[0] = x_ref[0] + acc


def out_stage(branch, proj, zx_col, qx_col, k_l, v_l, x, w_out_bf16, xq_g):
    bsz, seq, _ = branch.shape
    tm = min(OUT_TM, seq)
    return pl.pallas_call(
        _out_body,
        grid=(bsz, seq // tm),
        in_specs=[
            pl.BlockSpec((1, tm, D_INNER), lambda b, s: (b, s, 0)),
            pl.BlockSpec((1, tm, D_X), lambda b, s: (b, s, zx_col)),
            pl.BlockSpec((1, tm, D_X), lambda b, s: (b, s, qx_col)),
            pl.BlockSpec((1, MEM_LEN, D_X), lambda b, s: (b, 0, 0)),
            pl.BlockSpec((1, MEM_LEN, D_X), lambda b, s: (b, 0, 0)),
            pl.BlockSpec((1, tm, D_MODEL), lambda b, s: (b, s, 0)),
            pl.BlockSpec((D_GATE, D_MODEL), lambda b, s: (0, 0)),
            pl.BlockSpec((1, X_HEAD_DIM), lambda b, s: (0, 0)),
        ],
        out_specs=pl.BlockSpec((1, tm, D_MODEL), lambda b, s: (b, s, 0)),
        out_shape=jax.ShapeDtypeStruct((bsz, seq, D_MODEL), F32),
        compiler_params=_cparams(("parallel", "parallel")),
        name="out_stage",
    )(branch, proj, proj, k_l, v_l, x, w_out_bf16, xq_g.reshape(1, X_HEAD_DIM).astype(F32))


def _s5_scan_body(u_ref, b_ref, c_ref, a_ref, y_ref, hd_ref, st_ref, *, ts, bt):
    half = S5_STATE * (LANES // S5_GROUP)

    @pl.when(pl.program_id(2) == 0)
    def _():
        st_ref[...] = jnp.zeros_like(st_ref)

    u = u_ref[0].reshape(ts * bt, LANES)
    hd_ref[...] = jnp.dot(u, b_ref[0], preferred_element_type=F32)
    a = a_ref[0]
    ar, ai = a[:, :half], a[:, half:]
    st = st_ref[...]

    def step(t, carry):
        hr, hi = carry
        r0 = pl.multiple_of(t * bt, bt)
        d = hd_ref[pl.ds(r0, bt), :]
        nhr = ar * hr - ai * hi + d[:, :half]
        nhi = ar * hi + ai * hr + d[:, half:]
        hd_ref[pl.ds(r0, bt), :half] = nhr
        hd_ref[pl.ds(r0, bt), half:] = nhi
        return nhr, nhi

    hr, hi = lax.fori_loop(0, ts, step, (st[:, :half], st[:, half:]), unroll=8)
    st_ref[:, :half] = hr
    st_ref[:, half:] = hi
    y = jnp.dot(hd_ref[...].astype(BF16), c_ref[0], preferred_element_type=F32)
    y_ref[0] = y.reshape(ts, bt, LANES).astype(y_ref.dtype)


def s5_scan(u_t, b_blk, c_blk, a_blk):
    nsl, seq, bsz, _ = u_t.shape
    bt = S5_BT
    ts = min(S5_TS, seq)
    assert bsz % bt == 0 and seq % ts == 0
    ncol = 2 * S5_STATE * (LANES // S5_GROUP)
    return pl.pallas_call(
        functools.partial(_s5_scan_body, ts=ts, bt=bt),
        grid=(bsz // bt, nsl, seq // ts),
        in_specs=[
            pl.BlockSpec((1, ts, bt, LANES), lambda b, j, s: (j, s, b, 0)),
            pl.BlockSpec((1, LANES, ncol), lambda b, j, s: (j, 0, 0)),
            pl.BlockSpec((1, ncol, LANES), lambda b, j, s: (j, 0, 0)),
            pl.BlockSpec((1, bt, ncol), lambda b, j, s: (j, 0, 0)),
        ],
        out_specs=pl.BlockSpec((1, ts, bt, LANES), lambda b, j, s: (j, s, b, 0)),
        out_shape=jax.ShapeDtypeStruct((nsl, seq, bsz, LANES), BF16),
        scratch_shapes=[pltpu.VMEM((ts * bt, ncol), F32), pltpu.VMEM((bt, ncol), F32)],
        compiler_params=_cparams(("parallel", "parallel", "arbitrary")),
        name="s5_scan",
    )(u_t, b_blk, c_blk, a_blk)


def _s5_post_body(y_ref, u_ref, z_ref, d_ref, w_ref, o_ref):
    yv = y_ref[0].astype(F32) + d_ref[...] * u_ref[0].astype(F32)
    g = 0.5 * yv * (1.0 + jnp.tanh(math.sqrt(2.0 / math.pi) * (yv + 0.044715 * (yv * yv * yv))))
    gate = _sigmoid(jnp.dot(g.astype(BF16), w_ref[...], preferred_element_type=F32))
    o_ref[0] = (g * gate * _silu(z_ref[0].astype(F32))).astype(o_ref.dtype)


def s5_post(y, proj, d_skip, w_glu_bf16):
    bsz, seq, _ = y.shape
    tm = min(OUT_TM, seq)
    return pl.pallas_call(
        _s5_post_body,
        grid=(bsz, seq // tm),
        in_specs=[
            pl.BlockSpec((1, tm, D_INNER), lambda b, s: (b, s, 0)),
            pl.BlockSpec((1, tm, D_INNER), lambda b, s: (b, s, 0)),
            pl.BlockSpec((1, tm, D_INNER), lambda b, s: (b, s, 1)),
            pl.BlockSpec((1, D_INNER), lambda b, s: (0, 0)),
            pl.BlockSpec((D_INNER, D_INNER), lambda b, s: (0, 0)),
        ],
        out_specs=pl.BlockSpec((1, tm, D_INNER), lambda b, s: (b, s, 0)),
        out_shape=jax.ShapeDtypeStruct((bsz, seq, D_INNER), BF16),
        compiler_params=_cparams(("parallel", "parallel")),
        name="s5_post",
    )(y, proj, proj, d_skip.reshape(1, D_INNER).astype(F32), w_glu_bf16)


def s5_params(lam_re, lam_im, log_step, b_re, b_im, c_re, c_im):
    gps = LANES // S5_GROUP
    step = jnp.exp(log_step.astype(F32))[:, None]
    lam_re = lam_re.astype(F32)
    lam_im = lam_im.astype(F32)
    mag = jnp.exp(lam_re * step)
    lb_re = mag * jnp.cos(lam_im * step)
    lb_im = mag * jnp.sin(lam_im * step)
    den = lam_re * lam_re + lam_im * lam_im
    nr = lb_re - 1.0
    co_re = (nr * lam_re + lb_im * lam_im) / den
    co_im = (lb_im * lam_re - nr * lam_im) / den
    b_re = b_re.astype(F32)
    b_im = b_im.astype(F32)
    bb_re = co_re[..., None] * b_re - co_im[..., None] * b_im
    bb_im = co_re[..., None] * b_im + co_im[..., None] * b_re
    eye = jnp.eye(gps, dtype=F32)

    def blk_b(bb):
        t = bb.reshape(S5_SLICES, gps, S5_STATE, S5_GROUP)
        return jnp.einsum('jgph,gk->jghkp', t, eye).reshape(S5_SLICES, LANES, gps * S5_STATE)

    def blk_c(cc):
        t = cc.astype(F32).reshape(S5_SLICES, gps, S5_GROUP, S5_STATE)
        return jnp.einsum('jghp,gk->jkpgh', t, eye).reshape(S5_SLICES, gps * S5_STATE, LANES)

    b_blk = jnp.concatenate([blk_b(bb_re), blk_b(bb_im)], axis=-1).astype(BF16)
    c_blk = jnp.concatenate([blk_c(c_re), -blk_c(c_im)], axis=1).astype(BF16)
    a = jnp.concatenate([lb_re.reshape(S5_SLICES, gps * S5_STATE), lb_im.reshape(S5_SLICES, gps * S5_STATE)], axis=-1)
    a_blk = jnp.broadcast_to(a[:, None, :], (S5_SLICES, S5_BT, a.shape[-1]))
    return b_blk, c_blk, a_blk


def _gla_body(q_ref, k_ref, v_ref, z_ref, gk_ref, w2_ref, b2_ref, ng_ref, o_ref, st_ref, *, ts):
    C, R = GLA_CHUNK, GLA_SUB

    @pl.when(pl.program_id(2) == 0)
    def _():
        st_ref[...] = jnp.zeros_like(st_ref)

    row = lax.broadcasted_iota(jnp.int32, (C, C), 0)
    col = lax.broadcasted_iota(jnp.int32, (C, C), 1)
    tri = (col <= row).astype(F32)
    row_r = lax.broadcasted_iota(jnp.int32, (R, C), 0)
    col_r = lax.broadcasted_iota(jnp.int32, (R, C), 1)

    for c in range(ts // C):
        rs = slice(c * C, (c + 1) * C)
        x = jnp.dot(gk_ref[0, rs, :], w2_ref[...], precision=HIGHEST, preferred_element_type=F32) + b2_ref[...]
        g = (jnp.minimum(x, 0.0) - jnp.log(1.0 + jnp.exp(-jnp.abs(x)))) * (1.0 / GLA_TAU)
        b = jnp.dot(tri, g, precision=HIGHEST, preferred_element_type=F32)
        q = q_ref[0, rs, :].astype(F32) * (GLA_DK ** -0.5)
        k = k_ref[0, rs, :].astype(F32)
        v = v_ref[0, rs, :]
        state = st_ref[...]
        o = jnp.dot((q * jnp.exp(b)).astype(BF16), state.astype(BF16), preferred_element_type=F32)

        att_rows = []
        for i in range(C // R):
            bi = b[i * R:(i + 1) * R, :]
            qi = q[i * R:(i + 1) * R, :]
            ki = k[i * R:(i + 1) * R, :]
            dg = jnp.zeros((R, C), F32)
            for s in range(R):
                w = jnp.exp(jnp.minimum(bi - bi[s:s + 1, :], 0.0))
                cs = jnp.sum(qi * ki[s:s + 1, :] * w, axis=-1, keepdims=True)
                dg = jnp.where((col_r == i * R + s) & (row_r >= s), cs, dg)
            if i > 0:
                ref = b[i * R - 1:i * R, :]
                qs = qi * jnp.exp(bi - ref)
                ks = k * jnp.exp(jnp.minimum(ref - b, 0.0))
                a_off = lax.dot_general(qs.astype(BF16), ks.astype(BF16), (((1,), (1,)), ((), ())),
                                        preferred_element_type=F32)
                dg = dg + jnp.where(col_r < i * R, a_off, 0.0)
            att_rows.append(dg)
        att = jnp.concatenate(att_rows, axis=0)
        o = o + jnp.dot(att.astype(BF16), v, preferred_element_type=F32)

        b_last = b[C - 1:C, :]
        kd = k * jnp.exp(b_last - b)
        dec = jnp.broadcast_to(jnp.exp(b_last), (C, GLA_DK)).T
        upd = jnp.dot(kd.T.astype(BF16), v, preferred_element_type=F32)
        st_ref[...] = state * jnp.concatenate([dec] * (GLA_DV // GLA_DK), axis=1) + upd

        on = o * lax.rsqrt(jnp.mean(o * o, axis=-1, keepdims=True) + EPS) * ng_ref[...]
        o_ref[0, rs, :] = (on * _silu(z_ref[0, rs, :].astype(F32))).astype(o_ref.dtype)


def gla_mixer(proj, gk_aux, w_gk2, b_gk2, norm_g):
    bsz, seq, _ = proj.shape
    ts = min(GLA_TS, seq)
    w2 = jnp.zeros((LANES, GLA_HEADS * GLA_DK), F32).at[:GLA_RANK].set(w_gk2.astype(F32))
    return pl.pallas_call(
        functools.partial(_gla_body, ts=ts),
        grid=(bsz, GLA_HEADS, seq // ts),
        in_specs=[
            pl.BlockSpec((1, ts, GLA_DK), lambda b, h, s: (b, s, h)),
            pl.BlockSpec((1, ts, GLA_DK), lambda b, h, s: (b, s, GLA_HEADS + h)),
            pl.BlockSpec((1, ts, GLA_DV), lambda b, h, s: (b, s, 2 + h)),
            pl.BlockSpec((1, ts, GLA_DV), lambda b, h, s: (b, s, 6 + h)),
            pl.BlockSpec((1, ts, LANES), lambda b, h, s: (b, s, 0)),
            pl.BlockSpec((LANES, GLA_DK), lambda b, h, s: (0, h)),
            pl.BlockSpec((1, GLA_DK), lambda b, h, s: (0, h)),
            pl.BlockSpec((1, GLA_DV), lambda b, h, s: (0, 0)),
        ],
        out_specs=pl.BlockSpec((1, ts, GLA_DV), lambda b, h, s: (b, s, h)),
        out_shape=jax.ShapeDtypeStruct((bsz, seq, D_INNER), BF16),
        scratch_shapes=[pltpu.VMEM((GLA_DK, GLA_DV), F32)],
        compiler_params=_cparams(("parallel", "parallel", "arbitrary")),
        name="gla_mixer",
    )(proj, proj, proj, proj, gk_aux, w2, b_gk2.reshape(1, -1).astype(F32), norm_g.reshape(1, -1).astype(F32))


def _diff_prep_body(x_ref, g_ref, cos_ref, sin_ref, o_ref):
    lane = lax.broadcasted_iota(jnp.int32, (1, LANES), 1)
    first = (lane % DIFF_HALF) < (DIFF_HALF // 2)
    gr = lax.broadcasted_iota(jnp.int32, (LANES, LANES), 0) // DIFF_HALF
    gc = lax.broadcasted_iota(jnp.int32, (LANES, LANES), 1) // DIFF_HALF
    gmat = jnp.where(gr == gc, 1.0 / DIFF_HALF, 0.0).astype(BF16)
    cos = cos_ref[0]
    sin = jnp.where(first, -sin_ref[0], sin_ref[0])
    g = g_ref[0]
    for c in range(D_INNER // LANES):
        x = x_ref[0, :, c * LANES:(c + 1) * LANES].astype(F32)
        x2 = x * x
        x2h = x2.astype(BF16)
        x2l = (x2 - x2h.astype(F32)).astype(BF16)
        ms = jnp.dot(x2h, gmat, preferred_element_type=F32) + jnp.dot(x2l, gmat, preferred_element_type=F32)
        xn = x * lax.rsqrt(ms + EPS) * g
        partner = jnp.where(first, pltpu.roll(xn, LANES - DIFF_HALF // 2, 1), pltpu.roll(xn, DIFF_HALF // 2, 1))
        o_ref[0, 0, :, c * LANES:(c + 1) * LANES] = (xn * cos + partner * sin).astype(o_ref.dtype)


def diff_prep(proj, gains, cos_t, sin_t):
    bsz, seq, _ = proj.shape
    tm = min(OUT_TM, seq)
    return pl.pallas_call(
        _diff_prep_body,
        grid=(2, bsz, seq // tm),
        in_specs=[
            pl.BlockSpec((1, tm, D_INNER), lambda w, b, s: (b, s, w)),
            pl.BlockSpec((1, 1, LANES), lambda w, b, s: (w, 0, 0)),
            pl.BlockSpec((1, tm, LANES), lambda w, b, s: (b, s, 0)),
            pl.BlockSpec((1, tm, LANES), lambda w, b, s: (b, s, 0)),
        ],
        out_specs=pl.BlockSpec((1, 1, tm, D_INNER), lambda w, b, s: (w, b, s, 0)),
        out_shape=jax.ShapeDtypeStruct((2, bsz, seq, D_INNER), BF16),
        compiler_params=_cparams(("parallel", "parallel", "parallel")),
        name="diff_prep",
    )(proj, gains, cos_t, sin_t)


def _diff_attn_body(q_ref, k_ref, v_ref, z_ref, lam_ref, sg_ref, o_ref, *, tq, tk, scale_out):
    qi = pl.program_id(2)
    q = q_ref[0, 0]
    lane = lax.broadcasted_iota(jnp.int32, (tq, LANES), 1)
    zero = jnp.zeros_like(q)
    qq = jnp.concatenate([jnp.where(lane < DIFF_HALF, q, zero), jnp.where(lane >= DIFF_HALF, q, zero)], axis=0)

    def block(j, carry, masked):
        m, l, acc = carry
        k0 = pl.multiple_of(j * tk, tk)
        kb = k_ref[0, 0, pl.ds(k0, tk), :]
        vb = v_ref[0, pl.ds(k0, tk), :]
        s = lax.dot_general(qq, kb, (((1,), (1,)), ((), ())), preferred_element_type=F32)
        if masked:
            r = lax.broadcasted_iota(jnp.int32, (2 * tq, tk), 0)
            r = jnp.where(r >= tq, r - tq, r)
            cidx = lax.broadcasted_iota(jnp.int32, (2 * tq, tk), 1)
            s = jnp.where(cidx <= r, s, -jnp.inf)
        m_new = jnp.maximum(m, jnp.max(s, axis=-1, keepdims=True))
        alpha = jnp.exp(m - m_new)
        p = jnp.exp(s - m_new)
        l = alpha * l + jnp.sum(p, axis=-1, keepdims=True)
        acc = alpha * acc + jnp.dot(p.astype(BF16), vb, preferred_element_type=F32)
        return m_new, l, acc

    init = (jnp.full((2 * tq, 1), -jnp.inf, F32), jnp.zeros((2 * tq, 1), F32), jnp.zeros((2 * tq, LANES), F32))
    carry = lax.fori_loop(0, qi * (tq // tk), lambda j, cr: block(j, cr, False), init)
    for d in range(tq // tk):
        if tq == tk:
            carry = block(qi, carry, True)
        else:
            raise NotImplementedError
    m, l, acc = carry
    on = acc / l
    o = on[:tq] - lam_ref[...] * on[tq:]
    o = o * lax.rsqrt(jnp.mean(o * o, axis=-1, keepdims=True) + EPS) * sg_ref[...] * scale_out
    o_ref[0] = (o * _silu(z_ref[0].astype(F32))).astype(o_ref.dtype)


def diff_attn(qk, proj, lam_vec, subln_g, lam_init):
    _, bsz, seq, _ = qk.shape
    tq = min(DIFF_TQ, seq)
    tk = tq
    voff = 2 * D_INNER // LANES
    zoff = 3 * D_INNER // LANES
    return pl.pallas_call(
        functools.partial(_diff_attn_body, tq=tq, tk=tk, scale_out=1.0 - lam_init),
        grid=(bsz, DIFF_HEADS, seq // tq),
        in_specs=[
            pl.BlockSpec((1, 1, tq, LANES), lambda b, h, i: (0, b, i, h)),
            pl.BlockSpec((1, 1, seq, LANES), lambda b, h, i: (1, b, 0, h)),
            pl.BlockSpec((1, seq, LANES), lambda b, h, i: (b, 0, voff + h)),
            pl.BlockSpec((1, tq, LANES), lambda b, h, i: (b, i, zoff + h)),
            pl.BlockSpec((1, LANES), lambda b, h, i: (0, 0)),
            pl.BlockSpec((1, LANES), lambda b, h, i: (0, 0)),
        ],
        out_specs=pl.BlockSpec((1, tq, LANES), lambda b, h, i: (b, i, h)),
        out_shape=jax.ShapeDtypeStruct((bsz, seq, D_INNER), BF16),
        compiler_params=_cparams(("parallel", "parallel", "arbitrary")),
        name="diff_attn",
    )(qk, qk, proj, proj, lam_vec, subln_g.reshape(1, LANES).astype(F32))


def _ssd_body(xbc_ref, z_ref, dt_ref, cw_ref, cb_ref, dtb_ref, alog_ref, dsk_ref, ng_ref, e64_ref, e512_ref,
              o_ref, xs_ref, st_ref, *, ts):
    L = SSD_CHUNK
    gw = SSD_HPG * SSD_HEAD_DIM
    pad = 8

    @pl.when(pl.program_id(1) == 0)
    def _():
        st_ref[...] = jnp.zeros_like(st_ref)
        xs_ref[0:pad, :] = jnp.zeros((pad, SSD_CONV_CH), F32)

    @pl.when(pl.program_id(1) > 0)
    def _():
        xs_ref[0:pad, :] = xs_ref[ts:ts + pad, :]

    xs_ref[pad:pad + ts, :] = xbc_ref[0].astype(F32)
    cw = cw_ref[...]
    acc = cb_ref[...] + xs_ref[pad:pad + ts, :] * cw[SSD_CONV - 1:SSD_CONV, :]
    for w in range(SSD_CONV - 1):
        sh = SSD_CONV - 1 - w
        acc = acc + xs_ref[pad - sh:pad - sh + ts, :] * cw[w:w + 1, :]
    xc_all = _silu(acc)

    row = lax.broadcasted_iota(jnp.int32, (L, L), 0)
    col = lax.broadcasted_iota(jnp.int32, (L, L), 1)
    tri = (col <= row).astype(F32)
    row4 = lax.broadcasted_iota(jnp.int32, (L, SSD_HPG * L), 0)
    col4 = lax.broadcasted_iota(jnp.int32, (L, SSD_HPG * L), 1) % L
    eye4 = (row4 == col4).astype(F32)
    causal4 = col4 <= row4
    br = lax.broadcasted_iota(jnp.int32, (SSD_HPG * L, gw), 0) // L
    bc = lax.broadcasted_iota(jnp.int32, (SSD_HPG * L, gw), 1) // SSD_HEAD_DIM
    blockdiag = br == bc
    a_neg = -jnp.exp(alog_ref[...])

    for c in range(ts // L):
        rs = slice(c * L, (c + 1) * L)
        xc = xc_all[rs, :]
        xr = dt_ref[0, rs, :] + dtb_ref[...]
        dt = jnp.maximum(xr, 0.0) + jnp.log(1.0 + jnp.exp(-jnp.abs(xr)))
        cum = jnp.dot(tri, dt * a_neg, precision=HIGHEST, preferred_element_type=F32)
        cum64 = jnp.dot(cum, e64_ref[...], precision=HIGHEST, preferred_element_type=F32)
        dt64 = jnp.dot(dt, e64_ref[...], precision=HIGHEST, preferred_element_type=F32)
        cum512 = jnp.dot(cum, e512_ref[...], precision=HIGHEST, preferred_element_type=F32)
        last64 = cum64[L - 1:L, :]
        ys = []
        for g in range(SSD_GROUPS):
            xg = xc[:, g * gw:(g + 1) * gw]
            bg = xc[:, D_INNER + g * SSD_STATE:D_INNER + (g + 1) * SSD_STATE]
            cg = xc[:, D_INNER + SSD_GROUPS * SSD_STATE + g * SSD_STATE:
                    D_INNER + SSD_GROUPS * SSD_STATE + (g + 1) * SSD_STATE]
            cgb = cg.astype(BF16)
            cb = lax.dot_general(cgb, bg.astype(BF16), (((1,), (1,)), ((), ())), preferred_element_type=F32)
            ce = cum512[:, g * SSD_HPG * L:(g + 1) * SSD_HPG * L]
            crow = jnp.sum(ce * eye4, axis=0, keepdims=True)
            m = jnp.where(causal4, jnp.exp(jnp.minimum(ce - crow, 0.0)), 0.0) * jnp.concatenate([cb] * SSD_HPG, axis=1)
            cg64 = cum64[:, g * gw:(g + 1) * gw]
            dg64 = dt64[:, g * gw:(g + 1) * gw]
            xdt = xg * dg64
            rhs = jnp.where(blockdiag, jnp.concatenate([xdt] * SSD_HPG, axis=0), 0.0)
            y = jnp.dot(m.astype(BF16), rhs.astype(BF16), preferred_element_type=F32)
            state = st_ref[g]
            y = y + jnp.dot(cgb, state.astype(BF16), preferred_element_type=F32) * jnp.exp(cg64)
            lg = last64[:, g * gw:(g + 1) * gw]
            xw = xg * (jnp.exp(lg - cg64) * dg64)
            st_ref[g] = state * jnp.exp(lg) + jnp.dot(bg.T.astype(BF16), xw.astype(BF16), preferred_element_type=F32)
            ys.append(y + dsk_ref[:, g * gw:(g + 1) * gw] * xg)
        y = jnp.concatenate(ys, axis=1) * _silu(z_ref[0, rs, :].astype(F32))
        y = y * lax.rsqrt(jnp.mean(y * y, axis=-1, keepdims=True) + EPS) * ng_ref[...]
        o_ref[0, rs, :] = y.astype(o_ref.dtype)


def _ssd_expanders():
    e64 = np.zeros((LANES, D_INNER), np.float32)
    for h in range(SSD_HEADS):
        e64[h, h * SSD_HEAD_DIM:(h + 1) * SSD_HEAD_DIM] = 1.0
    e512 = np.zeros((LANES, SSD_HEADS * SSD_CHUNK), np.float32)
    for h in range(SSD_HEADS):
        e512[h, h * SSD_CHUNK:(h + 1) * SSD_CHUNK] = 1.0
    return jnp.asarray(e64), jnp.asarray(e512)


def ssd_mixer(proj, dt_aux, conv_w, conv_b, dt_bias, a_log, d_skip, norm_g):
    bsz, seq, _ = proj.shape
    ts = min(SSD_TS, seq)
    e64, e512 = _ssd_expanders()

    def pad128(v):
        return jnp.zeros((1, LANES), F32).at[0, :SSD_HEADS].set(v.astype(F32))

    dsk = jnp.repeat(d_skip.astype(F32), SSD_HEAD_DIM).reshape(1, D_INNER)
    full = lambda shape: pl.BlockSpec(shape, lambda b, s: (0,) * len(shape))
    return pl.pallas_call(
        functools.partial(_ssd_body, ts=ts),
        grid=(bsz, seq // ts),
        in_specs=[
            pl.BlockSpec((1, ts, SSD_CONV_CH), lambda b, s: (b, s, 0)),
            pl.BlockSpec((1, ts, D_INNER), lambda b, s: (b, s, 2)),
            pl.BlockSpec((1, ts, LANES), lambda b, s: (b, s, 0)),
            full((SSD_CONV, SSD_CONV_CH)),
            full((1, SSD_CONV_CH)),
            full((1, LANES)),
            full((1, LANES)),
            full((1, D_INNER)),
            full((1, D_INNER)),
            full((LANES, D_INNER)),
            full((LANES, SSD_HEADS * SSD_CHUNK)),
        ],
        out_specs=pl.BlockSpec((1, ts, D_INNER), lambda b, s: (b, s, 0)),
        out_shape=jax.ShapeDtypeStruct((bsz, seq, D_INNER), BF16),
        scratch_shapes=[pltpu.VMEM((ts + 16, SSD_CONV_CH), F32),
                        pltpu.VMEM((SSD_GROUPS, SSD_STATE, SSD_HPG * SSD_HEAD_DIM), F32)],
        compiler_params=_cparams(("parallel", "arbitrary")),
        name="ssd_mixer",
    )(proj, proj, dt_aux, conv_w.astype(F32), conv_b.reshape(1, -1).astype(F32), pad128(dt_bias), pad128(a_log),
      dsk, norm_g.reshape(1, -1).astype(F32), e64, e512)


def _pad_cols(w, width=LANES):
    return jnp.zeros((w.shape[0], width), F32).at[:, :w.shape[1]].set(w.astype(F32))


def s5_layer(x, norm_g, w_in, lam_re, lam_im, log_step, b_re, b_im, c_re, c_im, d_skip, w_glu):
    bsz, seq, d = x.shape
    proj = in_proj(x.reshape(bsz * seq, d), norm_g, w_in.astype(BF16)).reshape(bsz, seq, -1)
    b_blk, c_blk, a_blk = s5_params(lam_re, lam_im, log_step, b_re, b_im, c_re, c_im)
    u_t = proj[:, :, :D_INNER].reshape(bsz, seq, S5_SLICES, LANES).transpose(2, 1, 0, 3)
    y_t = s5_scan(u_t, b_blk, c_blk, a_blk)
    y = y_t.transpose(2, 1, 0, 3).reshape(bsz, seq, D_INNER)
    branch = s5_post(y, proj, d_skip, w_glu.astype(BF16))
    return branch, proj, (2 * D_INNER) // D_X, (2 * D_INNER) // D_X + 1


def gla_layer(x, norm_g, w_in, w_gk2, b_gk2, gla_norm_g):
    bsz, seq, d = x.shape
    nqk = GLA_HEADS * GLA_DK
    c0 = 2 * nqk + D_INNER
    w_main = jnp.concatenate([w_in[:, :c0], w_in[:, c0 + GLA_RANK:]], axis=1).astype(BF16)
    proj, gk_aux = in_proj(x.reshape(bsz * seq, d), norm_g, w_main, _pad_cols(w_in[:, c0:c0 + GLA_RANK]))
    proj = proj.reshape(bsz, seq, -1)
    branch = gla_mixer(proj, gk_aux.reshape(bsz, seq, LANES), w_gk2, b_gk2, gla_norm_g)
    zx_col = (c0 + D_INNER) // D_X
    return branch, proj, zx_col, zx_col + 1


def diff_layer(x, norm_g, w_in, cos_t, sin_t, q_g, k_g, lq1, lk1, lq2, lk2, subln_g, lam_init):
    bsz, seq, d = x.shape
    proj = in_proj(x.reshape(bsz * seq, d), norm_g, w_in.astype(BF16)).reshape(bsz, seq, -1)
    gains = jnp.stack([jnp.tile(q_g.astype(F32), LANES // DIFF_HALF) * (DIFF_HALF ** -0.5),
                       jnp.tile(k_g.astype(F32), LANES // DIFF_HALF)]).reshape(2, 1, LANES)
    qk = diff_prep(proj, gains, cos_t, sin_t)
    lam = (jnp.exp(jnp.sum(lq1.astype(F32) * lk1.astype(F32)))
           - jnp.exp(jnp.sum(lq2.astype(F32) * lk2.astype(F32))) + lam_init)
    lam_vec = jnp.full((1, LANES), lam, F32)
    branch = diff_attn(qk, proj, lam_vec, subln_g, lam_init)
    zx_col = (4 * D_INNER) // D_X
    return branch, proj, zx_col, zx_col + 1


def ssd_layer(x, norm_g, w_in, conv_w, conv_b, dt_bias, a_log, d_skip, ssd_norm_g):
    bsz, seq, d = x.shape
    c0 = SSD_CONV_CH
    w_main = jnp.concatenate([w_in[:, :c0], w_in[:, c0 + SSD_HEADS:]], axis=1).astype(BF16)
    proj, dt_aux = in_proj(x.reshape(bsz * seq, d), norm_g, w_main, _pad_cols(w_in[:, c0:c0 + SSD_HEADS]))
    proj = proj.reshape(bsz, seq, -1)
    branch = ssd_mixer(proj, dt_aux.reshape(bsz, seq, LANES), conv_w, conv_b, dt_bias, a_log, d_skip, ssd_norm_g)
    zx_col = (c0 + D_INNER) // D_X
    return branch, proj, zx_col, zx_col + 1


def rope_tables(positions):
    inv = ROPE_THETA ** (-jnp.arange(0, DIFF_HALF, 2, dtype=F32) / DIFF_HALF)
    ang = positions.astype(F32)[..., None] * inv
    reps = LANES // (DIFF_HALF // 2)
    return jnp.tile(jnp.cos(ang), (1, 1, reps)), jnp.tile(jnp.sin(ang), (1, 1, reps))


def kernel(x, mem, positions, norm_g, w_out, mem_norm_g, w_mem_kv, xq_g, xk_g, s5_w_in, s5_lam_re, s5_lam_im, s5_log_step, s5_b_re, s5_b_im, s5_c_re, s5_c_im, s5_d, s5_w_glu, gla_w_in, gla_w_gk2, gla_b_gk2, gla_norm_g, diff_w_in, diff_q_g, diff_k_g, diff_lq1, diff_lk1, diff_lq2, diff_lk2, diff_subln_g, ssd_w_in, ssd_conv_w, ssd_conv_b, ssd_dt_bias, ssd_a_log, ssd_d, ssd_norm_g):
    depth = norm_g.shape[0]
    k_all, v_all = mem_kv(mem, mem_norm_g, w_mem_kv, xk_g)
    cos_t, sin_t = rope_tables(positions)
    w_out_b = w_out.astype(BF16)
    for i in range(depth):
        kind, j = i % 4, i // 4
        if kind == 0:
            branch, proj, zc, qc = s5_layer(x, norm_g[i], s5_w_in[j], s5_lam_re[j], s5_lam_im[j], s5_log_step[j],
                                            s5_b_re[j], s5_b_im[j], s5_c_re[j], s5_c_im[j], s5_d[j], s5_w_glu[j])
        elif kind == 1:
            branch, proj, zc, qc = gla_layer(x, norm_g[i], gla_w_in[j], gla_w_gk2[j], gla_b_gk2[j], gla_norm_g[j])
        elif kind == 2:
            lam_init = 0.8 - 0.6 * math.exp(-0.3 * i)
            branch, proj, zc, qc = diff_layer(x, norm_g[i], diff_w_in[j], cos_t, sin_t, diff_q_g[j], diff_k_g[j],
                                              diff_lq1[j], diff_lk1[j], diff_lq2[j], diff_lk2[j], diff_subln_g[j],
                                              lam_init)
        else:
            branch, proj, zc, qc = ssd_layer(x, norm_g[i], ssd_w_in[j], ssd_conv_w[j], ssd_conv_b[j], ssd_dt_bias[j],
                                             ssd_a_log[j], ssd_d[j], ssd_norm_g[j])
        x = out_stage(branch, proj, zc, qc, k_all[i], v_all[i], x, w_out_b[i], xq_g[i])
    return x
```

```python
import functools
import math

import numpy as np
import jax
import jax.numpy as jnp
from jax import lax
from jax.experimental import pallas as pl
from jax.experimental.pallas import tpu as pltpu

F32 = jnp.float32
BF16 = jnp.bfloat16

D_MODEL = 1024
D_INNER = 2048
EPS = 1e-6
MEM_LEN = 256
X_HEADS = 4
X_HEAD_DIM = 128
D_X = X_HEADS * X_HEAD_DIM
D_GATE = D_INNER + D_X
S5_GROUP = 16
S5_GROUPS = D_INNER // S5_GROUP
S5_STATE = 64
GLA_HEADS = 4
GLA_DK = 128
GLA_DV = 512
GLA_RANK = 16
GLA_TAU = 16.0
DIFF_HEADS = 16
DIFF_HALF = 64
DIFF_VDIM = 128
ROPE_THETA = 10000.0
SSD_HEAD_DIM = 64
SSD_HEADS = 32
SSD_GROUPS = 8
SSD_HPG = 4
SSD_STATE = 128
SSD_CONV = 4
SSD_CONV_CH = D_INNER + 2 * SSD_GROUPS * SSD_STATE

LANES = 128
VMEM_LIMIT = 48 * 1024 * 1024

PROJ_TM = 1024
PROJ_TN = 1024
OUT_TM = 512
S5_BT = 16
S5_T = 16
GLA_TS = 256
GLA_CHUNK = 128
GLA_SUB = 16
GLA_SAFE_DECAY = 40.0
DIFF_TQ = 512
DIFF_DIRECT_MAX_LOG2 = 64.0
BF16_ROUNDING_SLACK = 1.02
SSD_TS = 256
SSD_CHUNK = 128
SSD_REPS = 3


def _cparams(sem):
    return pltpu.CompilerParams(dimension_semantics=sem, vmem_limit_bytes=VMEM_LIMIT)


def _sigmoid(x):
    return 1.0 / (1.0 + jnp.exp(-x))


def _silu(x):
    return x * _sigmoid(x)


def _split3(x):
    x1 = x.astype(BF16)
    r1 = x - x1.astype(F32)
    x2 = r1.astype(BF16)
    x3 = (r1 - x2.astype(F32)).astype(BF16)
    return x1, x2, x3


def _cumsum_rows(tri3, x):
    return jnp.dot(tri3, jnp.concatenate(_split3(x), axis=0), preferred_element_type=F32)


def _pack_split3(x, lane_grp):
    x1, x2, x3 = _split3(x)
    zero = jnp.zeros_like(x1)
    return jnp.where(lane_grp == 0, x1, jnp.where(lane_grp == 1, x2, jnp.where(lane_grp == 2, x3, zero)))


def _tri3(n):
    row = lax.broadcasted_iota(jnp.int32, (n, n), 0)
    col = lax.broadcasted_iota(jnp.int32, (n, n), 1)
    tri = (col <= row).astype(BF16)
    return jnp.concatenate([tri, tri, tri], axis=1)


def _in_proj_body(x_ref, g_ref, w_ref, *rest, has_aux):
    if has_aux:
        waux_ref, o_ref, oaux_ref, h_ref = rest
    else:
        o_ref, h_ref = rest

    @pl.when(pl.program_id(1) == 0)
    def _():
        x = x_ref[...]
        h = x * lax.rsqrt(jnp.mean(x * x, axis=-1, keepdims=True) + EPS) * g_ref[...]
        hb = h.astype(BF16)
        h_ref[...] = hb
        if has_aux:
            oaux_ref[...] = jnp.dot(hb, waux_ref[...], preferred_element_type=F32)

    o_ref[...] = jnp.dot(h_ref[...], w_ref[...], preferred_element_type=F32).astype(o_ref.dtype)


def in_proj(x2d, g, w_bf16, w_aux=None):
    n, d = x2d.shape
    nc = w_bf16.shape[1]
    tm = min(PROJ_TM, n)
    tn = PROJ_TN
    assert n % tm == 0 and nc % tn == 0
    has_aux = w_aux is not None
    in_specs = [
        pl.BlockSpec((tm, d), lambda i, j: (i, 0)),
        pl.BlockSpec((1, d), lambda i, j: (0, 0)),
        pl.BlockSpec((d, tn), lambda i, j: (0, j)),
    ]
    args = [x2d, g.reshape(1, d).astype(F32), w_bf16]
    out_shape = [jax.ShapeDtypeStruct((n, nc), BF16)]
    out_specs = [pl.BlockSpec((tm, tn), lambda i, j: (i, j))]
    if has_aux:
        in_specs.append(pl.BlockSpec((d, LANES), lambda i, j: (0, 0)))
        args.append(w_aux.astype(BF16))
        out_shape.append(jax.ShapeDtypeStruct((n, LANES), F32))
        out_specs.append(pl.BlockSpec((tm, LANES), lambda i, j: (i, 0)))
    res = pl.pallas_call(
        functools.partial(_in_proj_body, has_aux=has_aux),
        grid=(n // tm, nc // tn),
        in_specs=in_specs,
        out_specs=out_specs,
        out_shape=out_shape,
        scratch_shapes=[pltpu.VMEM((tm, d), BF16)],
        compiler_params=_cparams(("parallel", "arbitrary")),
        name="in_proj_aux" if has_aux else "in_proj",
    )(*args)
    return (res[0], res[1]) if has_aux else res[0]


def _mem_kv_body(mem_ref, mg_ref, w_ref, kg_ref, k_ref, v_ref):
    m = mem_ref[0]
    mn = m * lax.rsqrt(jnp.mean(m * m, axis=-1, keepdims=True) + EPS) * mg_ref[...]
    kv = jnp.dot(mn.astype(BF16), w_ref[0], preferred_element_type=F32)
    kg = kg_ref[0]
    for h in range(X_HEADS):
        kh = kv[:, h * X_HEAD_DIM:(h + 1) * X_HEAD_DIM]
        kh = kh * lax.rsqrt(jnp.mean(kh * kh, axis=-1, keepdims=True) + EPS) * kg
        k_ref[0, 0, :, h * X_HEAD_DIM:(h + 1) * X_HEAD_DIM] = kh.astype(BF16)
    v_ref[0, 0] = kv[:, D_X:].astype(BF16)


def mem_kv(mem, mem_norm_g, w_mem_kv, xk_g):
    bsz, mlen, d = mem.shape
    depth = w_mem_kv.shape[0]
    return pl.pallas_call(
        _mem_kv_body,
        grid=(depth, bsz),
        in_specs=[
            pl.BlockSpec((1, mlen, d), lambda l, b: (b, 0, 0)),
            pl.BlockSpec((1, d), lambda l, b: (0, 0)),
            pl.BlockSpec((1, d, 2 * D_X), lambda l, b: (l, 0, 0)),
            pl.BlockSpec((1, 1, X_HEAD_DIM), lambda l, b: (l, 0, 0)),
        ],
        out_specs=[
            pl.BlockSpec((1, 1, mlen, D_X), lambda l, b: (l, b, 0, 0)),
            pl.BlockSpec((1, 1, mlen, D_X), lambda l, b: (l, b, 0, 0)),
        ],
        out_shape=[jax.ShapeDtypeStruct((depth, bsz, mlen, D_X), BF16)] * 2,
        compiler_params=_cparams(("parallel", "parallel")),
        name="mem_kv",
    )(mem, mem_norm_g.reshape(1, d).astype(F32), w_mem_kv.astype(BF16),
      xk_g.reshape(depth, 1, X_HEAD_DIM).astype(F32))


def _out_body(br_ref, zx_ref, qx_ref, k_ref, v_ref, x_ref, w_ref, qg_ref, o_ref):
    q = qx_ref[0].astype(F32)
    zx = zx_ref[0].astype(F32)
    qg = qg_ref[...]
    outs = []
    for h in range(X_HEADS):
        sl = slice(h * X_HEAD_DIM, (h + 1) * X_HEAD_DIM)
        qh = q[:, sl]
        qn = qh * lax.rsqrt(jnp.mean(qh * qh, axis=-1, keepdims=True) + EPS) * qg
        s = lax.dot_general(qn.astype(BF16), k_ref[0, :, sl], (((1,), (1,)), ((), ())),
                            preferred_element_type=F32) * (X_HEAD_DIM ** -0.5)
        p = jnp.exp(s - jnp.max(s, axis=-1, keepdims=True))
        l = jnp.sum(p, axis=-1, keepdims=True)
        oh = jnp.dot(p.astype(BF16), v_ref[0, :, sl], preferred_element_type=F32) / l
        outs.append((oh * _silu(zx[:, sl])).astype(BF16))
    mem_out = jnp.concatenate(outs, axis=-1)
    acc = jnp.dot(br_ref[0], w_ref[:D_INNER, :], preferred_element_type=F32)
    acc = acc + jnp.dot(mem_out, w_ref[D_INNER:, :], preferred_element_type=F32)
    o_ref[0] = x_ref[0] + acc


def out_stage(branch, proj, zx_col, qx_col, k_l, v_l, x, w_out_bf16, xq_g):
    bsz, seq, _ = branch.shape
    tm = min(OUT_TM, seq)
    return pl.pallas_call(
        _out_body,
        grid=(bsz, seq // tm),
        in_specs=[
            pl.BlockSpec((1, tm, D_INNER), lambda b, s: (b, s, 0)),
            pl.BlockSpec((1, tm, D_X), lambda b, s: (b, s, zx_col)),
            pl.BlockSpec((1, tm, D_X), lambda b, s: (b, s, qx_col)),
            pl.BlockSpec((1, MEM_LEN, D_X), lambda b, s: (b, 0, 0)),
            pl.BlockSpec((1, MEM_LEN, D_X), lambda b, s: (b, 0, 0)),
            pl.BlockSpec((1, tm, D_MODEL), lambda b, s: (b, s, 0)),
            pl.BlockSpec((D_GATE, D_MODEL), lambda b, s: (0, 0)),
            pl.BlockSpec((1, X_HEAD_DIM), lambda b, s: (0, 0)),
        ],
        out_specs=pl.BlockSpec((1, tm, D_MODEL), lambda b, s: (b, s, 0)),
        out_shape=jax.ShapeDtypeStruct((bsz, seq, D_MODEL), F32),
        compiler_params=_cparams(("parallel", "parallel")),
        name="out_stage",
    )(branch, proj, proj, k_l, v_l, x, w_out_bf16, xq_g.reshape(1, X_HEAD_DIM).astype(F32))


def _s5_chunk_body(u_ref, toep_ref, m_ref, n_ref, a_ref, y_ref, xs_ref, sp_ref, *, nchunk, bt):
    gw = S5_T * S5_GROUP
    u = u_ref[0].reshape(nchunk * bt, 2 * gw)
    xs_ref[...] = jnp.dot(u, m_ref[0], preferred_element_type=F32)
    a = a_ref[0]
    a_re, a_im = a[:, :LANES], a[:, LANES:]

    def step(c, carry):
        sr, si = carry
        r0 = pl.multiple_of(c * bt, bt)
        sp_ref[pl.ds(r0, bt), :LANES] = sr
        sp_ref[pl.ds(r0, bt), LANES:] = si
        x = xs_ref[pl.ds(r0, bt), :]
        return a_re * sr - a_im * si + x[:, :LANES], a_re * si + a_im * sr + x[:, LANES:]

    zero = jnp.zeros((bt, LANES), F32)
    lax.fori_loop(0, nchunk, step, (zero, zero), unroll=8)
    y = jnp.dot(sp_ref[...].astype(BF16), n_ref[0], preferred_element_type=F32)
    y = y + jnp.concatenate([jnp.dot(u[:, :gw], toep_ref[0, 0], preferred_element_type=F32),
                             jnp.dot(u[:, gw:], toep_ref[0, 1], preferred_element_type=F32)], axis=1)
    y_ref[0] = y.reshape(nchunk, bt, 2 * gw).astype(y_ref.dtype)


def s5_chunked(u_p, toep, m_pair, n_pair, a_pair):
    npair, nchunk, bsz, w = u_p.shape
    bt = S5_BT
    assert bsz % bt == 0
    return pl.pallas_call(
        functools.partial(_s5_chunk_body, nchunk=nchunk, bt=bt),
        grid=(npair, bsz // bt),
        in_specs=[
            pl.BlockSpec((1, nchunk, bt, w), lambda p, b: (p, 0, b, 0)),
            pl.BlockSpec((1, 2, w // 2, w // 2), lambda p, b: (p, 0, 0, 0)),
            pl.BlockSpec((1, w, 2 * LANES), lambda p, b: (p, 0, 0)),
            pl.BlockSpec((1, 2 * LANES, w), lambda p, b: (p, 0, 0)),
            pl.BlockSpec((1, bt, 2 * LANES), lambda p, b: (p, 0, 0)),
        ],
        out_specs=pl.BlockSpec((1, nchunk, bt, w), lambda p, b: (p, 0, b, 0)),
        out_shape=jax.ShapeDtypeStruct(u_p.shape, BF16),
        scratch_shapes=[pltpu.VMEM((nchunk * bt, 2 * LANES), F32), pltpu.VMEM((nchunk * bt, 2 * LANES), F32)],
        compiler_params=_cparams(("parallel", "parallel")),
        name="s5_chunked",
    )(u_p, toep, m_pair, n_pair, a_pair)


def s5_chunk_params(lam_re, lam_im, log_step, b_re, b_im, c_re, c_im):
    T, G, P, H = S5_T, S5_GROUPS, S5_STATE, S5_GROUP
    hp = lax.Precision.HIGHEST
    step = jnp.exp(log_step.astype(F32))[:, None]
    lam_re = lam_re.astype(F32)
    lam_im = lam_im.astype(F32)
    mag = jnp.exp(lam_re * step)
    lb_re = mag * jnp.cos(lam_im * step)
    lb_im = mag * jnp.sin(lam_im * step)
    den = lam_re * lam_re + lam_im * lam_im
    nr = lb_re - 1.0
    co_re = (nr * lam_re + lb_im * lam_im) / den
    co_im = (lb_im * lam_re - nr * lam_im) / den
    b_re = b_re.astype(F32)
    b_im = b_im.astype(F32)
    bb_re = co_re[..., None] * b_re - co_im[..., None] * b_im
    bb_im = co_re[..., None] * b_im + co_im[..., None] * b_re
    c_re = c_re.astype(F32)
    c_im = c_im.astype(F32)
    j = jnp.arange(T + 1, dtype=F32)[:, None, None]
    pmag = jnp.exp(j * (lam_re * step))
    pw_re = pmag * jnp.cos(j * (lam_im * step))
    pw_im = pmag * jnp.sin(j * (lam_im * step))
    cl_re = c_re[None] * pw_re[:, :, None, :] - c_im[None] * pw_im[:, :, None, :]
    cl_im = c_re[None] * pw_im[:, :, None, :] + c_im[None] * pw_re[:, :, None, :]
    kj = (jnp.einsum('jgkp,gph->jgkh', cl_re, bb_re, precision=hp)
          - jnp.einsum('jgkp,gph->jgkh', cl_im, bb_im, precision=hp))
    lag = jnp.arange(T)[None, :] - jnp.arange(T)[:, None]
    toep = jnp.where((lag >= 0)[:, :, None, None, None], kj[jnp.clip(lag, 0, T)], 0.0)
    toep = toep.transpose(2, 0, 4, 1, 3).reshape(G // 2, 2, T * H, T * H)
    rev = pw_re[T - 1 - jnp.arange(T)], pw_im[T - 1 - jnp.arange(T)]
    m_re = rev[0][..., None] * bb_re[None] - rev[1][..., None] * bb_im[None]
    m_im = rev[0][..., None] * bb_im[None] + rev[1][..., None] * bb_re[None]
    m_re = m_re.transpose(1, 0, 3, 2).reshape(G // 2, 2, T * H, P)
    m_im = m_im.transpose(1, 0, 3, 2).reshape(G // 2, 2, T * H, P)
    z = jnp.zeros_like(m_re[:, 0])
    m_pair = jnp.concatenate([
        jnp.concatenate([m_re[:, 0], z, m_im[:, 0], z], axis=-1),
        jnp.concatenate([z, m_re[:, 1], z, m_im[:, 1]], axis=-1)], axis=1)
    n_re = cl_re[1:].transpose(1, 3, 0, 2).reshape(G // 2, 2, P, T * H)
    n_im = (-cl_im[1:]).transpose(1, 3, 0, 2).reshape(G // 2, 2, P, T * H)
    zn = jnp.zeros_like(n_re[:, 0])
    n_pair = jnp.concatenate([
        jnp.concatenate([n_re[:, 0], zn], axis=-1), jnp.concatenate([zn, n_re[:, 1]], axis=-1),
        jnp.concatenate([n_im[:, 0], zn], axis=-1), jnp.concatenate([zn, n_im[:, 1]], axis=-1)], axis=1)
    a = jnp.concatenate([pw_re[T].reshape(G // 2, 2 * P), pw_im[T].reshape(G // 2, 2 * P)], axis=-1)
    a_pair = jnp.broadcast_to(a[:, None, :], (G // 2, S5_BT, 4 * P))
    return toep.astype(BF16), m_pair.astype(BF16), n_pair.astype(BF16), a_pair


def _s5_post_body(y_ref, u_ref, z_ref, d_ref, w_ref, o_ref):
    yv = y_ref[0].astype(F32) + d_ref[...] * u_ref[0].astype(F32)
    g = 0.5 * yv * (1.0 + jnp.tanh(math.sqrt(2.0 / math.pi) * (yv + 0.044715 * (yv * yv * yv))))
    gate = _sigmoid(jnp.dot(g.astype(BF16), w_ref[...], preferred_element_type=F32))
    o_ref[0] = (g * gate * _silu(z_ref[0].astype(F32))).astype(o_ref.dtype)


def s5_post(y, proj, d_skip, w_glu_bf16):
    bsz, seq, _ = y.shape
    tm = min(OUT_TM, seq)
    return pl.pallas_call(
        _s5_post_body,
        grid=(bsz, seq // tm),
        in_specs=[
            pl.BlockSpec((1, tm, D_INNER), lambda b, s: (b, s, 0)),
            pl.BlockSpec((1, tm, D_INNER), lambda b, s: (b, s, 0)),
            pl.BlockSpec((1, tm, D_INNER), lambda b, s: (b, s, 1)),
            pl.BlockSpec((1, D_INNER), lambda b, s: (0, 0)),
            pl.BlockSpec((D_INNER, D_INNER), lambda b, s: (0, 0)),
        ],
        out_specs=pl.BlockSpec((1, tm, D_INNER), lambda b, s: (b, s, 0)),
        out_shape=jax.ShapeDtypeStruct((bsz, seq, D_INNER), BF16),
        compiler_params=_cparams(("parallel", "parallel")),
        name="s5_post",
    )(y, proj, proj, d_skip.reshape(1, D_INNER).astype(F32), w_glu_bf16)


def _gla_body(q_ref, k_ref, v_ref, z_ref, gk_ref, w2_ref, b2_ref, ng_ref, o_ref, st_ref, att_ref, *, ts):
    C, R = GLA_CHUNK, GLA_SUB
    nsub = C // R
    nt = (((1,), (1,)), ((), ()))

    @pl.when(pl.program_id(2) == 0)
    def _():
        st_ref[...] = jnp.zeros_like(st_ref)

    tri3 = _tri3(C)
    row_r = lax.broadcasted_iota(jnp.int32, (R, C), 0)
    col_r = lax.broadcasted_iota(jnp.int32, (R, C), 1)

    for c in range(ts // C):
        rs = slice(c * C, (c + 1) * C)
        x = jnp.dot(gk_ref[0, rs, :].astype(BF16), w2_ref[...], preferred_element_type=F32) + b2_ref[...]
        g = (jnp.minimum(x, 0.0) - jnp.log(1.0 + jnp.exp(-jnp.abs(x)))) * (1.0 / GLA_TAU)
        b = _cumsum_rows(tri3, g)
        q = q_ref[0, rs, :].astype(F32) * (GLA_DK ** -0.5)
        k = k_ref[0, rs, :].astype(F32)
        v = v_ref[0, rs, :]
        state = st_ref[...]
        o = jnp.dot((q * jnp.exp(b)).astype(BF16), state.astype(BF16), preferred_element_type=F32)

        sub_decay = jnp.max(-jnp.sum(g.reshape(nsub, R, GLA_DK), axis=1))

        def scores_matmul(i, cap):
            bi = b[i * R:(i + 1) * R, :]
            ref = b[i * R - 1:i * R, :] if i > 0 else jnp.zeros((1, GLA_DK), F32)
            qs = q[i * R:(i + 1) * R, :] * jnp.exp(bi - ref)
            ks = k * jnp.exp(jnp.minimum(ref - b, cap))
            return lax.dot_general(qs.astype(BF16), ks.astype(BF16), nt, preferred_element_type=F32)

        @pl.when(sub_decay <= GLA_SAFE_DECAY)
        def _():
            for i in range(nsub):
                att_ref[i * R:(i + 1) * R, :] = jnp.where(col_r <= row_r + i * R, scores_matmul(i, GLA_SAFE_DECAY), 0.0)

        @pl.when(sub_decay > GLA_SAFE_DECAY)
        def _():
            for i in range(nsub):
                bi = b[i * R:(i + 1) * R, :]
                qi = q[i * R:(i + 1) * R, :]
                ki = k[i * R:(i + 1) * R, :]
                dg = jnp.zeros((R, C), F32)
                for s in range(R):
                    w = jnp.exp(jnp.minimum(bi - bi[s:s + 1, :], 0.0))
                    cs = jnp.sum(qi * ki[s:s + 1, :] * w, axis=-1, keepdims=True)
                    dg = jnp.where((col_r == i * R + s) & (row_r >= s), cs, dg)
                if i > 0:
                    dg = dg + jnp.where(col_r < i * R, scores_matmul(i, 0.0), 0.0)
                att_ref[i * R:(i + 1) * R, :] = dg

        o = o + jnp.dot(att_ref[...].astype(BF16), v, preferred_element_type=F32)

        b_last = b[C - 1:C, :]
        kd = k * jnp.exp(b_last - b)
        dec = jnp.broadcast_to(jnp.exp(b_last), (C, GLA_DK)).T
        upd = jnp.dot(kd.T.astype(BF16), v, preferred_element_type=F32)
        st_ref[...] = state * jnp.concatenate([dec] * (GLA_DV // GLA_DK), axis=1) + upd

        on = o * lax.rsqrt(jnp.mean(o * o, axis=-1, keepdims=True) + EPS) * ng_ref[...]
        o_ref[0, rs, :] = (on * _silu(z_ref[0, rs, :].astype(F32))).astype(o_ref.dtype)


def gla_mixer(proj, gk_aux, w_gk2, b_gk2, norm_g):
    bsz, seq, _ = proj.shape
    ts = min(GLA_TS, seq)
    w2 = jnp.zeros((LANES, GLA_HEADS * GLA_DK), F32).at[:GLA_RANK].set(w_gk2.astype(F32)).astype(BF16)
    return pl.pallas_call(
        functools.partial(_gla_body, ts=ts),
        grid=(bsz, GLA_HEADS, seq // ts),
        in_specs=[
            pl.BlockSpec((1, ts, GLA_DK), lambda b, h, s: (b, s, h)),
            pl.BlockSpec((1, ts, GLA_DK), lambda b, h, s: (b, s, GLA_HEADS + h)),
            pl.BlockSpec((1, ts, GLA_DV), lambda b, h, s: (b, s, 2 + h)),
            pl.BlockSpec((1, ts, GLA_DV), lambda b, h, s: (b, s, 6 + h)),
            pl.BlockSpec((1, ts, LANES), lambda b, h, s: (b, s, 0)),
            pl.BlockSpec((LANES, GLA_DK), lambda b, h, s: (0, h)),
            pl.BlockSpec((1, GLA_DK), lambda b, h, s: (0, h)),
            pl.BlockSpec((1, GLA_DV), lambda b, h, s: (0, 0)),
        ],
        out_specs=pl.BlockSpec((1, ts, GLA_DV), lambda b, h, s: (b, s, h)),
        out_shape=jax.ShapeDtypeStruct((bsz, seq, D_INNER), BF16),
        scratch_shapes=[pltpu.VMEM((GLA_DK, GLA_DV), F32), pltpu.VMEM((GLA_CHUNK, GLA_CHUNK), F32)],
        compiler_params=_cparams(("parallel", "parallel", "arbitrary")),
        name="gla_mixer",
    )(proj, proj, proj, proj, gk_aux, w2, b_gk2.reshape(1, -1).astype(F32), norm_g.reshape(1, -1).astype(F32))


def _diff_prep_body(x_ref, g_ref, cos_ref, sin_ref, o_ref):
    lane = lax.broadcasted_iota(jnp.int32, (1, LANES), 1)
    first = (lane % DIFF_HALF) < (DIFF_HALF // 2)
    gr = lax.broadcasted_iota(jnp.int32, (LANES, LANES), 0) // DIFF_HALF
    gc = lax.broadcasted_iota(jnp.int32, (LANES, LANES), 1) // DIFF_HALF
    gmat = jnp.where(gr == gc, 1.0 / DIFF_HALF, 0.0).astype(BF16)
    cos = cos_ref[0]
    sin = jnp.where(first, -sin_ref[0], sin_ref[0])
    g = g_ref[0]
    for c in range(D_INNER // LANES):
        x = x_ref[0, :, c * LANES:(c + 1) * LANES].astype(F32)
        x2 = x * x
        x2h = x2.astype(BF16)
        x2l = (x2 - x2h.astype(F32)).astype(BF16)
        ms = jnp.dot(x2h, gmat, preferred_element_type=F32) + jnp.dot(x2l, gmat, preferred_element_type=F32)
        xn = x * lax.rsqrt(ms + EPS) * g
        partner = jnp.where(first, pltpu.roll(xn, LANES - DIFF_HALF // 2, 1), pltpu.roll(xn, DIFF_HALF // 2, 1))
        o_ref[0, 0, :, c * LANES:(c + 1) * LANES] = (xn * cos + partner * sin).astype(o_ref.dtype)


def diff_prep(proj, gains, cos_t, sin_t):
    bsz, seq, _ = proj.shape
    tm = min(OUT_TM, seq)
    return pl.pallas_call(
        _diff_prep_body,
        grid=(2, bsz, seq // tm),
        in_specs=[
            pl.BlockSpec((1, tm, D_INNER), lambda w, b, s: (b, s, w)),
            pl.BlockSpec((1, 1, LANES), lambda w, b, s: (w, 0, 0)),
            pl.BlockSpec((1, tm, LANES), lambda w, b, s: (b, s, 0)),
            pl.BlockSpec((1, tm, LANES), lambda w, b, s: (b, s, 0)),
        ],
        out_specs=pl.BlockSpec((1, 1, tm, D_INNER), lambda w, b, s: (w, b, s, 0)),
        out_shape=jax.ShapeDtypeStruct((2, bsz, seq, D_INNER), BF16),
        compiler_params=_cparams(("parallel", "parallel", "parallel")),
        name="diff_prep",
    )(proj, gains, cos_t, sin_t)


def _diff_epilogue(on, lam, sg, z, scale_out):
    r = on.shape[0] // 2
    o = on[:r] - lam * on[r:]
    o = o * lax.rsqrt(jnp.mean(o * o, axis=-1, keepdims=True) + EPS) * sg * scale_out
    return o * _silu(z.astype(F32))


def _diff_attn_online_body(q_ref, k_ref, v_ref, z_ref, lam_ref, sg_ref, o_ref, *, tq, scale_out):
    qi = pl.program_id(2)
    q = q_ref[0, 0]
    lane = lax.broadcasted_iota(jnp.int32, (tq, LANES), 1)
    zero = jnp.zeros_like(q)
    qq = jnp.concatenate([jnp.where(lane < DIFF_HALF, q, zero), jnp.where(lane >= DIFF_HALF, q, zero)], axis=0)

    def block(j, carry, masked):
        m, l, acc = carry
        k0 = pl.multiple_of(j * tq, tq)
        kb = k_ref[0, 0, pl.ds(k0, tq), :]
        vb = v_ref[0, pl.ds(k0, tq), :]
        s = lax.dot_general(qq, kb, (((1,), (1,)), ((), ())), preferred_element_type=F32)
        if masked:
            r = lax.broadcasted_iota(jnp.int32, (2 * tq, tq), 0)
            r = jnp.where(r >= tq, r - tq, r)
            cidx = lax.broadcasted_iota(jnp.int32, (2 * tq, tq), 1)
            s = jnp.where(cidx <= r, s, -jnp.inf)
        m_new = jnp.maximum(m, jnp.max(s, axis=-1, keepdims=True))
        alpha = jnp.exp2(m - m_new)
        p = jnp.exp2(s - m_new)
        l = alpha * l + jnp.sum(p, axis=-1, keepdims=True)
        acc = alpha * acc + jnp.dot(p.astype(BF16), vb, preferred_element_type=F32)
        return m_new, l, acc

    init = (jnp.full((2 * tq, 1), -jnp.inf, F32), jnp.zeros((2 * tq, 1), F32), jnp.zeros((2 * tq, LANES), F32))
    carry = lax.fori_loop(0, qi, lambda j, cr: block(j, cr, False), init)
    m, l, acc = block(qi, carry, True)
    o_ref[0] = _diff_epilogue(acc / l, lam_ref[...], sg_ref[...], z_ref[0], scale_out).astype(o_ref.dtype)


def _diff_attn_direct_body(q_ref, k_ref, v_ref, z_ref, lam_ref, sg_ref, o_ref, *, seq, scale_out):
    tq, hq = DIFF_TQ, DIFF_TQ // 2
    lane = lax.broadcasted_iota(jnp.int32, (tq, LANES), 1)
    tri = (lax.broadcasted_iota(jnp.int32, (2 * hq, hq), 1)
           <= lax.broadcasted_iota(jnp.int32, (2 * hq, hq), 0) % hq)
    nt = (((1,), (1,)), ((), ()))

    def lanesum(p):
        acc = p[:, :LANES]
        for c in range(1, p.shape[1] // LANES):
            acc = acc + p[:, c * LANES:(c + 1) * LANES]
        return acc

    def qtile(i):
        r0 = i * tq
        q = q_ref[0, 0, pl.ds(r0, tq), :]
        zero = jnp.zeros_like(q)
        q0 = jnp.where(lane < DIFF_HALF, q, zero)
        q1 = jnp.where(lane >= DIFF_HALF, q, zero)
        qq = jnp.concatenate([q0[:hq], q1[:hq], q0[hq:], q1[hq:]], axis=0)

        def offdiag(j, carry):
            acc, l = carry
            c0 = j * tq
            p = jnp.exp2(lax.dot_general(qq, k_ref[0, 0, pl.ds(c0, tq), :], nt, preferred_element_type=F32))
            return (acc + jnp.dot(p.astype(BF16), v_ref[0, pl.ds(c0, tq), :], preferred_element_type=F32),
                    l + lanesum(p))

        carry = (jnp.zeros((2 * tq, LANES), F32), jnp.zeros((2 * tq, LANES), F32))
        for j in range(i):
            carry = offdiag(j, carry)
        acc, l = carry
        r1 = r0 + hq
        k_a, v_a = k_ref[0, 0, pl.ds(r0, hq), :], v_ref[0, pl.ds(r0, hq), :]
        k_b, v_b = k_ref[0, 0, pl.ds(r1, hq), :], v_ref[0, pl.ds(r1, hq), :]
        qq_a, qq_b = qq[:tq], qq[tq:]
        p_aa = jnp.where(tri, jnp.exp2(lax.dot_general(qq_a, k_a, nt, preferred_element_type=F32)), 0.0)
        p_ba = jnp.exp2(lax.dot_general(qq_b, k_a, nt, preferred_element_type=F32))
        p_bb = jnp.where(tri, jnp.exp2(lax.dot_general(qq_b, k_b, nt, preferred_element_type=F32)), 0.0)
        acc_a = acc[:tq] + jnp.dot(p_aa.astype(BF16), v_a, preferred_element_type=F32)
        acc_b = (acc[tq:] + jnp.dot(p_ba.astype(BF16), v_a, preferred_element_type=F32)
                 + jnp.dot(p_bb.astype(BF16), v_b, preferred_element_type=F32))
        l_a = jnp.sum(l[:tq] + lanesum(p_aa), axis=-1, keepdims=True)
        l_b = jnp.sum(l[tq:] + lanesum(p_ba) + lanesum(p_bb), axis=-1, keepdims=True)
        o_ref[0, pl.ds(r0, hq), :] = _diff_epilogue(
            acc_a / l_a, lam_ref[...], sg_ref[...], z_ref[0, pl.ds(r0, hq), :], scale_out).astype(o_ref.dtype)
        o_ref[0, pl.ds(r1, hq), :] = _diff_epilogue(
            acc_b / l_b, lam_ref[...], sg_ref[...], z_ref[0, pl.ds(r1, hq), :], scale_out).astype(o_ref.dtype)

    for i in range(seq // tq):
        qtile(i)


def diff_attn(qk, proj, lam_vec, subln_g, lam_init, score_bound_log2):
    _, bsz, seq, _ = qk.shape
    voff = 2 * D_INNER // LANES
    zoff = 3 * D_INNER // LANES
    sg = subln_g.reshape(1, LANES).astype(F32)
    vec = pl.BlockSpec((1, LANES), lambda *_: (0, 0))

    def online(qk, proj, lam_vec, sg):
        tq = min(DIFF_TQ, seq)
        return pl.pallas_call(
            functools.partial(_diff_attn_online_body, tq=tq, scale_out=1.0 - lam_init),
            grid=(bsz, DIFF_HEADS, seq // tq),
            in_specs=[
                pl.BlockSpec((1, 1, tq, LANES), lambda b, h, i: (0, b, i, h)),
                pl.BlockSpec((1, 1, seq, LANES), lambda b, h, i: (1, b, 0, h)),
                pl.BlockSpec((1, seq, LANES), lambda b, h, i: (b, 0, voff + h)),
                pl.BlockSpec((1, tq, LANES), lambda b, h, i: (b, i, zoff + h)),
                vec, vec,
            ],
            out_specs=pl.BlockSpec((1, tq, LANES), lambda b, h, i: (b, i, h)),
            out_shape=jax.ShapeDtypeStruct((bsz, seq, D_INNER), BF16),
            compiler_params=_cparams(("parallel", "parallel", "arbitrary")),
            name="diff_attn_online",
        )(qk, qk, proj, proj, lam_vec, sg)

    def direct(qk, proj, lam_vec, sg):
        return pl.pallas_call(
            functools.partial(_diff_attn_direct_body, seq=seq, scale_out=1.0 - lam_init),
            grid=(bsz, DIFF_HEADS),
            in_specs=[
                pl.BlockSpec((1, 1, seq, LANES), lambda b, h: (0, b, 0, h)),
                pl.BlockSpec((1, 1, seq, LANES), lambda b, h: (1, b, 0, h)),
                pl.BlockSpec((1, seq, LANES), lambda b, h: (b, 0, voff + h)),
                pl.BlockSpec((1, seq, LANES), lambda b, h: (b, 0, zoff + h)),
                vec, vec,
            ],
            out_specs=pl.BlockSpec((1, seq, LANES), lambda b, h: (b, 0, h)),
            out_shape=jax.ShapeDtypeStruct((bsz, seq, D_INNER), BF16),
            compiler_params=_cparams(("parallel", "parallel")),
            name="diff_attn_direct",
        )(qk, qk, proj, proj, lam_vec, sg)

    if seq % DIFF_TQ != 0:
        return online(qk, proj, lam_vec, sg)
    return lax.cond(score_bound_log2 <= DIFF_DIRECT_MAX_LOG2, direct, online, qk, proj, lam_vec, sg)


def _ssd_body(xbc_ref, z_ref, dt_ref, cw_ref, cb_ref, dtb_ref, alog_ref, dsk_ref, ng_ref, e64_ref, e512_ref,
              o_ref, xs_ref, st_ref, *, ts):
    L = SSD_CHUNK
    gw = SSD_HPG * SSD_HEAD_DIM
    pad = 8

    @pl.when(pl.program_id(1) == 0)
    def _():
        st_ref[...] = jnp.zeros_like(st_ref)
        xs_ref[0:pad, :] = jnp.zeros((pad, SSD_CONV_CH), F32)

    @pl.when(pl.program_id(1) > 0)
    def _():
        xs_ref[0:pad, :] = xs_ref[ts:ts + pad, :]

    xs_ref[pad:pad + ts, :] = xbc_ref[0].astype(F32)
    cw = cw_ref[...]
    acc = cb_ref[...] + xs_ref[pad:pad + ts, :] * cw[SSD_CONV - 1:SSD_CONV, :]
    for w in range(SSD_CONV - 1):
        sh = SSD_CONV - 1 - w
        acc = acc + xs_ref[pad - sh:pad - sh + ts, :] * cw[w:w + 1, :]
    xc_all = _silu(acc)

    tri3 = _tri3(L)
    lane_grp = lax.broadcasted_iota(jnp.int32, (L, LANES), 1) // SSD_HEADS
    row4 = lax.broadcasted_iota(jnp.int32, (L, SSD_HPG * L), 0)
    col4 = lax.broadcasted_iota(jnp.int32, (L, SSD_HPG * L), 1) % L
    eye4 = (row4 == col4).astype(F32)
    causal4 = col4 <= row4
    br = lax.broadcasted_iota(jnp.int32, (SSD_HPG * L, gw), 0) // L
    bc = lax.broadcasted_iota(jnp.int32, (SSD_HPG * L, gw), 1) // SSD_HEAD_DIM
    blockdiag = br == bc
    a_neg = -jnp.exp(alog_ref[...])

    for c in range(ts // L):
        rs = slice(c * L, (c + 1) * L)
        xc = xc_all[rs, :]
        xr = dt_ref[0, rs, :] + dtb_ref[...]
        dt = jnp.maximum(xr, 0.0) + jnp.log(1.0 + jnp.exp(-jnp.abs(xr)))
        cum = _cumsum_rows(tri3, dt * a_neg)
        cum_p = _pack_split3(cum, lane_grp)
        cum64 = jnp.dot(cum_p, e64_ref[...], preferred_element_type=F32)
        dt64 = jnp.dot(_pack_split3(dt, lane_grp), e64_ref[...], preferred_element_type=F32)
        cum512 = jnp.dot(cum_p, e512_ref[...], preferred_element_type=F32)
        last64 = cum64[L - 1:L, :]
        ys = []
        for g in range(SSD_GROUPS):
            xg = xc[:, g * gw:(g + 1) * gw]
            bg = xc[:, D_INNER + g * SSD_STATE:D_INNER + (g + 1) * SSD_STATE]
            cg = xc[:, D_INNER + SSD_GROUPS * SSD_STATE + g * SSD_STATE:
                    D_INNER + SSD_GROUPS * SSD_STATE + (g + 1) * SSD_STATE]
            cgb = cg.astype(BF16)
            cb = lax.dot_general(cgb, bg.astype(BF16), (((1,), (1,)), ((), ())), preferred_element_type=F32)
            ce = cum512[:, g * SSD_HPG * L:(g + 1) * SSD_HPG * L]
            crow = jnp.sum(ce * eye4, axis=0, keepdims=True)
            m = jnp.where(causal4, jnp.exp(jnp.minimum(ce - crow, 0.0)), 0.0) * jnp.concatenate([cb] * SSD_HPG, axis=1)
            cg64 = cum64[:, g * gw:(g + 1) * gw]
            dg64 = dt64[:, g * gw:(g + 1) * gw]
            xdt = xg * dg64
            rhs = jnp.where(blockdiag, jnp.concatenate([xdt] * SSD_HPG, axis=0), 0.0)
            y = jnp.dot(m.astype(BF16), rhs.astype(BF16), preferred_element_type=F32)
            state = st_ref[g]
            y = y + jnp.dot(cgb, state.astype(BF16), preferred_element_type=F32) * jnp.exp(cg64)
            lg = last64[:, g * gw:(g + 1) * gw]
            xw = xg * (jnp.exp(lg - cg64) * dg64)
            st_ref[g] = state * jnp.exp(lg) + jnp.dot(bg.T.astype(BF16), xw.astype(BF16), preferred_element_type=F32)
            ys.append(y + dsk_ref[:, g * gw:(g + 1) * gw] * xg)
        y = jnp.concatenate(ys, axis=1) * _silu(z_ref[0, rs, :].astype(F32))
        y = y * lax.rsqrt(jnp.mean(y * y, axis=-1, keepdims=True) + EPS) * ng_ref[...]
        o_ref[0, rs, :] = y.astype(o_ref.dtype)


def _ssd_expanders():
    e64 = np.zeros((LANES, D_INNER), np.float32)
    e512 = np.zeros((LANES, SSD_HEADS * SSD_CHUNK), np.float32)
    for rep in range(SSD_REPS):
        for h in range(SSD_HEADS):
            e64[rep * SSD_HEADS + h, h * SSD_HEAD_DIM:(h + 1) * SSD_HEAD_DIM] = 1.0
            e512[rep * SSD_HEADS + h, h * SSD_CHUNK:(h + 1) * SSD_CHUNK] = 1.0
    return jnp.asarray(e64, BF16), jnp.asarray(e512, BF16)


def ssd_mixer(proj, dt_aux, conv_w, conv_b, dt_bias, a_log, d_skip, norm_g):
    bsz, seq, _ = proj.shape
    ts = min(SSD_TS, seq)
    e64, e512 = _ssd_expanders()

    def pad128(v):
        return jnp.zeros((1, LANES), F32).at[0, :SSD_REPS * SSD_HEADS].set(jnp.tile(v.astype(F32), SSD_REPS))

    dsk = jnp.repeat(d_skip.astype(F32), SSD_HEAD_DIM).reshape(1, D_INNER)
    full = lambda shape: pl.BlockSpec(shape, lambda b, s: (0,) * len(shape))
    return pl.pallas_call(
        functools.partial(_ssd_body, ts=ts),
        grid=(bsz, seq // ts),
        in_specs=[
            pl.BlockSpec((1, ts, SSD_CONV_CH), lambda b, s: (b, s, 0)),
            pl.BlockSpec((1, ts, D_INNER), lambda b, s: (b, s, 2)),
            pl.BlockSpec((1, ts, LANES), lambda b, s: (b, s, 0)),
            full((SSD_CONV, SSD_CONV_CH)),
            full((1, SSD_CONV_CH)),
            full((1, LANES)),
            full((1, LANES)),
            full((1, D_INNER)),
            full((1, D_INNER)),
            full((LANES, D_INNER)),
            full((LANES, SSD_HEADS * SSD_CHUNK)),
        ],
        out_specs=pl.BlockSpec((1, ts, D_INNER), lambda b, s: (b, s, 0)),
        out_shape=jax.ShapeDtypeStruct((bsz, seq, D_INNER), BF16),
        scratch_shapes=[pltpu.VMEM((ts + 16, SSD_CONV_CH), F32),
                        pltpu.VMEM((SSD_GROUPS, SSD_STATE, SSD_HPG * SSD_HEAD_DIM), F32)],
        compiler_params=_cparams(("parallel", "arbitrary")),
        name="ssd_mixer",
    )(proj, proj, dt_aux, conv_w.astype(F32), conv_b.reshape(1, -1).astype(F32), pad128(dt_bias), pad128(a_log),
      dsk, norm_g.reshape(1, -1).astype(F32), e64, e512)


def _pad_cols(w, width=LANES):
    return jnp.zeros((w.shape[0], width), F32).at[:, :w.shape[1]].set(w.astype(F32))


def s5_layer(x, norm_g, w_in, lam_re, lam_im, log_step, b_re, b_im, c_re, c_im, d_skip, w_glu):
    bsz, seq, d = x.shape
    proj = in_proj(x.reshape(bsz * seq, d), norm_g, w_in.astype(BF16)).reshape(bsz, seq, -1)
    toep, m_pair, n_pair, a_pair = s5_chunk_params(lam_re, lam_im, log_step, b_re, b_im, c_re, c_im)
    nchunk, npair = seq // S5_T, S5_GROUPS // 2
    u_p = (proj[:, :, :D_INNER].reshape(bsz, nchunk, S5_T, npair, 2, S5_GROUP)
           .transpose(3, 1, 0, 4, 2, 5).reshape(npair, nchunk, bsz, 2 * S5_T * S5_GROUP))
    y_p = s5_chunked(u_p, toep, m_pair, n_pair, a_pair)
    y = (y_p.reshape(npair, nchunk, bsz, 2, S5_T, S5_GROUP)
         .transpose(2, 1, 4, 0, 3, 5).reshape(bsz, seq, D_INNER))
    branch = s5_post(y, proj, d_skip, w_glu.astype(BF16))
    return branch, proj, (2 * D_INNER) // D_X, (2 * D_INNER) // D_X + 1


def gla_layer(x, norm_g, w_in, w_gk2, b_gk2, gla_norm_g):
    bsz, seq, d = x.shape
    nqk = GLA_HEADS * GLA_DK
    c0 = 2 * nqk + D_INNER
    w_main = jnp.concatenate([w_in[:, :c0], w_in[:, c0 + GLA_RANK:]], axis=1).astype(BF16)
    proj, gk_aux = in_proj(x.reshape(bsz * seq, d), norm_g, w_main, _pad_cols(w_in[:, c0:c0 + GLA_RANK]))
    proj = proj.reshape(bsz, seq, -1)
    branch = gla_mixer(proj, gk_aux.reshape(bsz, seq, LANES), w_gk2, b_gk2, gla_norm_g)
    zx_col = (c0 + D_INNER) // D_X
    return branch, proj, zx_col, zx_col + 1


def diff_layer(x, norm_g, w_in, cos_t, sin_t, q_g, k_g, lq1, lk1, lq2, lk2, subln_g, lam_init):
    bsz, seq, d = x.shape
    proj = in_proj(x.reshape(bsz * seq, d), norm_g, w_in.astype(BF16)).reshape(bsz, seq, -1)
    q_scale = (DIFF_HALF ** -0.5) * math.log2(math.e)
    gains = jnp.stack([jnp.tile(q_g.astype(F32), LANES // DIFF_HALF) * q_scale,
                       jnp.tile(k_g.astype(F32), LANES // DIFF_HALF)]).reshape(2, 1, LANES)
    qk = diff_prep(proj, gains, cos_t, sin_t)
    score_bound_log2 = (BF16_ROUNDING_SLACK * DIFF_HALF * q_scale
                        * jnp.max(jnp.abs(q_g.astype(F32))) * jnp.max(jnp.abs(k_g.astype(F32))))
    lam = (jnp.exp(jnp.sum(lq1.astype(F32) * lk1.astype(F32)))
           - jnp.exp(jnp.sum(lq2.astype(F32) * lk2.astype(F32))) + lam_init)
    lam_vec = jnp.full((1, LANES), lam, F32)
    branch = diff_attn(qk, proj, lam_vec, subln_g, lam_init, score_bound_log2)
    zx_col = (4 * D_INNER) // D_X
    return branch, proj, zx_col, zx_col + 1


def ssd_layer(x, norm_g, w_in, conv_w, conv_b, dt_bias, a_log, d_skip, ssd_norm_g):
    bsz, seq, d = x.shape
    c0 = SSD_CONV_CH
    w_main = jnp.concatenate([w_in[:, :c0], w_in[:, c0 + SSD_HEADS:]], axis=1).astype(BF16)
    w_dt = jnp.tile(w_in[:, c0:c0 + SSD_HEADS], (1, SSD_REPS))
    proj, dt_aux = in_proj(x.reshape(bsz * seq, d), norm_g, w_main, _pad_cols(w_dt))
    proj = proj.reshape(bsz, seq, -1)
    branch = ssd_mixer(proj, dt_aux.reshape(bsz, seq, LANES), conv_w, conv_b, dt_bias, a_log, d_skip, ssd_norm_g)
    zx_col = (c0 + D_INNER) // D_X
    return branch, proj, zx_col, zx_col + 1


def rope_tables(positions):
    inv = ROPE_THETA ** (-jnp.arange(0, DIFF_HALF, 2, dtype=F32) / DIFF_HALF)
    ang = positions.astype(F32)[..., None] * inv
    reps = LANES // (DIFF_HALF // 2)
    return jnp.tile(jnp.cos(ang), (1, 1, reps)), jnp.tile(jnp.sin(ang), (1, 1, reps))


def kernel(x, mem, positions, norm_g, w_out, mem_norm_g, w_mem_kv, xq_g, xk_g, s5_w_in, s5_lam_re, s5_lam_im, s5_log_step, s5_b_re, s5_b_im, s5_c_re, s5_c_im, s5_d, s5_w_glu, gla_w_in, gla_w_gk2, gla_b_gk2, gla_norm_g, diff_w_in, diff_q_g, diff_k_g, diff_lq1, diff_lk1, diff_lq2, diff_lk2, diff_subln_g, ssd_w_in, ssd_conv_w, ssd_conv_b, ssd_dt_bias, ssd_a_log, ssd_d, ssd_norm_g):
    depth = norm_g.shape[0]
    k_all, v_all = mem_kv(mem, mem_norm_g, w_mem_kv, xk_g)
    cos_t, sin_t = rope_tables(positions)
    w_out_b = w_out.astype(BF16)
    for i in range(depth):
        kind, j = i % 4, i // 4
        if kind == 0:
            branch, proj, zc, qc = s5_layer(x, norm_g[i], s5_w_in[j], s5_lam_re[j], s5_lam_im[j], s5_log_step[j],
                                            s5_b_re[j], s5_b_im[j], s5_c_re[j], s5_c_im[j], s5_d[j], s5_w_glu[j])
        elif kind == 1:
            branch, proj, zc, qc = gla_layer(x, norm_g[i], gla_w_in[j], gla_w_gk2[j], gla_b_gk2[j], gla_norm_g[j])
        elif kind == 2:
            lam_init = 0.8 - 0.6 * math.exp(-0.3 * i)
            branch, proj, zc, qc = diff_layer(x, norm_g[i], diff_w_in[j], cos_t, sin_t, diff_q_g[j], diff_k_g[j],
                                              diff_lq1[j], diff_lk1[j], diff_lq2[j], diff_lk2[j], diff_subln_g[j],
                                              lam_init)
        else:
            branch, proj, zc, qc = ssd_layer(x, norm_g[i], ssd_w_in[j], ssd_conv_w[j], ssd_conv_b[j], ssd_dt_bias[j],
                                             ssd_a_log[j], ssd_d[j], ssd_norm_g[j])
        x = out_stage(branch, proj, zc, qc, k_all[i], v_all[i], x, w_out_b[i], xq_g[i])
    return x
```

```python
import functools
import math

import numpy as np
import jax
import jax.numpy as jnp
from jax import lax
from jax.experimental import pallas as pl
from jax.experimental.pallas import tpu as pltpu

F32 = jnp.float32
BF16 = jnp.bfloat16

D_MODEL = 1024
D_INNER = 2048
EPS = 1e-6
MEM_LEN = 256
X_HEADS = 4
X_HEAD_DIM = 128
D_X = X_HEADS * X_HEAD_DIM
D_GATE = D_INNER + D_X
S5_GROUP = 16
S5_GROUPS = D_INNER // S5_GROUP
S5_STATE = 64
GLA_HEADS = 4
GLA_DK = 128
GLA_DV = 512
GLA_RANK = 16
GLA_TAU = 16.0
DIFF_HEADS = 16
DIFF_HALF = 64
DIFF_VDIM = 128
ROPE_THETA = 10000.0
SSD_HEAD_DIM = 64
SSD_HEADS = 32
SSD_GROUPS = 8
SSD_HPG = 4
SSD_STATE = 128
SSD_CONV = 4
SSD_CONV_CH = D_INNER + 2 * SSD_GROUPS * SSD_STATE

LANES = 128
MXU_WIDTH = 256
VMEM_LIMIT = 48 * 1024 * 1024

PROJ_TM = 1024
PROJ_TN_MAX = 2560
OUT_TM = 512
S5_BT = 16
S5_T = 16
S5_TS = 512
GLA_TS = 256
GLA_CHUNK = 128
GLA_SUB = 16
GLA_SAFE_DECAY = 40.0
DIFF_TQ = 512
DIFF_DIRECT_MAX_LOG2 = 64.0
BF16_ROUNDING_SLACK = 1.02
SSD_TS = 256
SSD_CHUNK = 128
SSD_REPS = 3


def _cparams(sem):
    return pltpu.CompilerParams(dimension_semantics=sem, vmem_limit_bytes=VMEM_LIMIT)


def _sigmoid(x):
    return 1.0 / (1.0 + jnp.exp(-x))


def _silu(x):
    return x * _sigmoid(x)


def _split3(x):
    x1 = x.astype(BF16)
    r1 = x - x1.astype(F32)
    x2 = r1.astype(BF16)
    x3 = (r1 - x2.astype(F32)).astype(BF16)
    return x1, x2, x3


def _cumsum_rows(tri3, x):
    return jnp.dot(tri3, jnp.concatenate(_split3(x), axis=0), preferred_element_type=F32)


def _pack_split3(x, lane_grp):
    x1, x2, x3 = _split3(x)
    zero = jnp.zeros_like(x1)
    return jnp.where(lane_grp == 0, x1, jnp.where(lane_grp == 1, x2, jnp.where(lane_grp == 2, x3, zero)))


def _tri3(n):
    row = lax.broadcasted_iota(jnp.int32, (n, n), 0)
    col = lax.broadcasted_iota(jnp.int32, (n, n), 1)
    tri = (col <= row).astype(BF16)
    return jnp.concatenate([tri, tri, tri], axis=1)


def _in_proj_body(x_ref, g_ref, w_ref, *rest, has_aux):
    if has_aux:
        waux_ref, o_ref, oaux_ref, h_ref = rest
    else:
        o_ref, h_ref = rest

    @pl.when(pl.program_id(1) == 0)
    def _():
        x = x_ref[...]
        h = x * lax.rsqrt(jnp.mean(x * x, axis=-1, keepdims=True) + EPS) * g_ref[...]
        hb = h.astype(BF16)
        h_ref[...] = hb
        if has_aux:
            oaux_ref[...] = jnp.dot(hb, waux_ref[...], preferred_element_type=F32)

    o_ref[...] = jnp.dot(h_ref[...], w_ref[...], preferred_element_type=F32).astype(o_ref.dtype)


def in_proj(x2d, g, w_bf16, w_aux=None):
    n, d = x2d.shape
    nc = w_bf16.shape[1]
    tm = min(PROJ_TM, n)
    tn = max(t for t in range(MXU_WIDTH, PROJ_TN_MAX + 1, MXU_WIDTH) if nc % t == 0)
    assert n % tm == 0
    has_aux = w_aux is not None
    in_specs = [
        pl.BlockSpec((tm, d), lambda i, j: (i, 0)),
        pl.BlockSpec((1, d), lambda i, j: (0, 0)),
        pl.BlockSpec((d, tn), lambda i, j: (0, j)),
    ]
    args = [x2d, g.reshape(1, d).astype(F32), w_bf16]
    out_shape = [jax.ShapeDtypeStruct((n, nc), BF16)]
    out_specs = [pl.BlockSpec((tm, tn), lambda i, j: (i, j))]
    if has_aux:
        in_specs.append(pl.BlockSpec((d, LANES), lambda i, j: (0, 0)))
        args.append(w_aux.astype(BF16))
        out_shape.append(jax.ShapeDtypeStruct((n, LANES), F32))
        out_specs.append(pl.BlockSpec((tm, LANES), lambda i, j: (i, 0)))
    res = pl.pallas_call(
        functools.partial(_in_proj_body, has_aux=has_aux),
        grid=(n // tm, nc // tn),
        in_specs=in_specs,
        out_specs=out_specs,
        out_shape=out_shape,
        scratch_shapes=[pltpu.VMEM((tm, d), BF16)],
        compiler_params=_cparams(("parallel", "arbitrary")),
        name="in_proj_aux" if has_aux else "in_proj",
    )(*args)
    return (res[0], res[1]) if has_aux else res[0]


def _mem_kv_body(mem_ref, mg_ref, w_ref, kg_ref, k_ref, v_ref):
    m = mem_ref[0]
    mn = m * lax.rsqrt(jnp.mean(m * m, axis=-1, keepdims=True) + EPS) * mg_ref[...]
    kv = jnp.dot(mn.astype(BF16), w_ref[0], preferred_element_type=F32)
    kg = kg_ref[0]
    for h in range(X_HEADS):
        kh = kv[:, h * X_HEAD_DIM:(h + 1) * X_HEAD_DIM]
        kh = kh * lax.rsqrt(jnp.mean(kh * kh, axis=-1, keepdims=True) + EPS) * kg
        k_ref[0, 0, :, h * X_HEAD_DIM:(h + 1) * X_HEAD_DIM] = kh.astype(BF16)
    v_ref[0, 0] = kv[:, D_X:].astype(BF16)


def mem_kv(mem, mem_norm_g, w_mem_kv, xk_g):
    bsz, mlen, d = mem.shape
    depth = w_mem_kv.shape[0]
    return pl.pallas_call(
        _mem_kv_body,
        grid=(depth, bsz),
        in_specs=[
            pl.BlockSpec((1, mlen, d), lambda l, b: (b, 0, 0)),
            pl.BlockSpec((1, d), lambda l, b: (0, 0)),
            pl.BlockSpec((1, d, 2 * D_X), lambda l, b: (l, 0, 0)),
            pl.BlockSpec((1, 1, X_HEAD_DIM), lambda l, b: (l, 0, 0)),
        ],
        out_specs=[
            pl.BlockSpec((1, 1, mlen, D_X), lambda l, b: (l, b, 0, 0)),
            pl.BlockSpec((1, 1, mlen, D_X), lambda l, b: (l, b, 0, 0)),
        ],
        out_shape=[jax.ShapeDtypeStruct((depth, bsz, mlen, D_X), BF16)] * 2,
        compiler_params=_cparams(("parallel", "parallel")),
        name="mem_kv",
    )(mem, mem_norm_g.reshape(1, d).astype(F32), w_mem_kv.astype(BF16),
      xk_g.reshape(depth, 1, X_HEAD_DIM).astype(F32))


def _out_body(br_ref, zx_ref, qx_ref, k_ref, v_ref, x_ref, w_ref, qg_ref, o_ref):
    q = qx_ref[0].astype(F32)
    zx = zx_ref[0].astype(F32)
    qg = qg_ref[...]
    outs = []
    for h in range(X_HEADS):
        sl = slice(h * X_HEAD_DIM, (h + 1) * X_HEAD_DIM)
        qh = q[:, sl]
        qn = qh * lax.rsqrt(jnp.mean(qh * qh, axis=-1, keepdims=True) + EPS) * qg
        s = lax.dot_general(qn.astype(BF16), k_ref[0, :, sl], (((1,), (1,)), ((), ())),
                            preferred_element_type=F32) * (X_HEAD_DIM ** -0.5)
        p = jnp.exp(s - jnp.max(s, axis=-1, keepdims=True))
        l = jnp.sum(p, axis=-1, keepdims=True)
        oh = jnp.dot(p.astype(BF16), v_ref[0, :, sl], preferred_element_type=F32) / l
        outs.append((oh * _silu(zx[:, sl])).astype(BF16))
    mem_out = jnp.concatenate(outs, axis=-1)
    acc = jnp.dot(br_ref[0], w_ref[:D_INNER, :], preferred_element_type=F32)
    acc = acc + jnp.dot(mem_out, w_ref[D_INNER:, :], preferred_element_type=F32)
    o_ref[0] = x_ref[0] + acc


def out_stage(branch, proj, zx_col, qx_col, k_l, v_l, x, w_out_bf16, xq_g):
    bsz, seq, _ = branch.shape
    tm = min(OUT_TM, seq)
    return pl.pallas_call(
        _out_body,
        grid=(bsz, seq // tm),
        in_specs=[
            pl.BlockSpec((1, tm, D_INNER), lambda b, s: (b, s, 0)),
            pl.BlockSpec((1, tm, D_X), lambda b, s: (b, s, zx_col)),
            pl.BlockSpec((1, tm, D_X), lambda b, s: (b, s, qx_col)),
            pl.BlockSpec((1, MEM_LEN, D_X), lambda b, s: (b, 0, 0)),
            pl.BlockSpec((1, MEM_LEN, D_X), lambda b, s: (b, 0, 0)),
            pl.BlockSpec((1, tm, D_MODEL), lambda b, s: (b, s, 0)),
            pl.BlockSpec((D_GATE, D_MODEL), lambda b, s: (0, 0)),
            pl.BlockSpec((1, X_HEAD_DIM), lambda b, s: (0, 0)),
        ],
        out_specs=pl.BlockSpec((1, tm, D_MODEL), lambda b, s: (b, s, 0)),
        out_shape=jax.ShapeDtypeStruct((bsz, seq, D_MODEL), F32),
        compiler_params=_cparams(("parallel", "parallel")),
        name="out_stage",
    )(branch, proj, proj, k_l, v_l, x, w_out_bf16, xq_g.reshape(1, X_HEAD_DIM).astype(F32))


def _s5_slab_body(u_ref, bd_ref, m_ref, n_ref, a_ref, y_ref, w_ref, xs_ref, sp_ref, st_ref, *, nc, bt):
    T = S5_T
    half = S5_STATE * (LANES // S5_GROUP)

    @pl.when((pl.program_id(1) == 0) & (pl.program_id(2) == 0))
    def _():
        for s in range(T):
            for t in range(max(s - 1, 0), T):
                blk = bd_ref[0, t - s] if t >= s else jnp.zeros((LANES, LANES), BF16)
                w_ref[s * LANES:(s + 1) * LANES, t * LANES:(t + 1) * LANES] = blk

    @pl.when(pl.program_id(2) == 0)
    def _():
        st_ref[...] = jnp.zeros_like(st_ref)

    v4 = pltpu.einshape("bcsl->csbl", u_ref[...].reshape(bt, nc, T, LANES))
    v = jnp.concatenate([v4[:, s].reshape(nc * bt, LANES) for s in range(T)], axis=1)
    xs_ref[...] = jnp.dot(v, m_ref[0], preferred_element_type=F32)
    a = a_ref[0]
    a_re, a_im = a[:, :half], a[:, half:]

    def step(c, carry):
        sr, si = carry
        r0 = pl.multiple_of(c * bt, bt)
        sp_ref[pl.ds(r0, bt), :half] = sr
        sp_ref[pl.ds(r0, bt), half:] = si
        x = xs_ref[pl.ds(r0, bt), :]
        return a_re * sr - a_im * si + x[:, :half], a_re * si + a_im * sr + x[:, half:]

    st = st_ref[...]
    sr, si = lax.fori_loop(0, nc, step, (st[:, :half], st[:, half:]), unroll=4)
    st_ref[:, :half] = sr
    st_ref[:, half:] = si
    y_state = jnp.dot(sp_ref[...].astype(BF16), n_ref[0], preferred_element_type=F32)
    cols = []
    for tp in range(T // 2):
        kk = (2 * tp + 2) * LANES
        cols.append(jnp.dot(v[:, :kk], w_ref[:kk, tp * 2 * LANES:(tp + 1) * 2 * LANES], preferred_element_type=F32))
    y = (y_state + jnp.concatenate(cols, axis=1)).astype(BF16)
    y4 = jnp.stack([y[:, t * LANES:(t + 1) * LANES].reshape(nc, bt, LANES) for t in range(T)], axis=1)
    y_ref[...] = pltpu.einshape("ctbl->bctl", y4).reshape(bt, nc * T, LANES)


def s5_slab(proj, bd, m_slab, n_slab, a_slab):
    bsz, seq, _ = proj.shape
    bt = S5_BT
    ts = min(S5_TS, seq)
    nslab = D_INNER // LANES
    nstate = 2 * S5_STATE * (LANES // S5_GROUP)
    assert bsz % bt == 0 and seq % ts == 0
    once = dict(pipeline_mode=pl.Buffered(1))
    return pl.pallas_call(
        functools.partial(_s5_slab_body, nc=ts // S5_T, bt=bt),
        grid=(nslab, bsz // bt, seq // ts),
        in_specs=[
            pl.BlockSpec((bt, ts, LANES), lambda j, b, s: (b, s, j)),
            pl.BlockSpec((1, S5_T, LANES, LANES), lambda j, b, s: (j, 0, 0, 0)),
            pl.BlockSpec((1, S5_T * LANES, nstate), lambda j, b, s: (j, 0, 0), **once),
            pl.BlockSpec((1, nstate, S5_T * LANES), lambda j, b, s: (j, 0, 0), **once),
            pl.BlockSpec((1, bt, nstate), lambda j, b, s: (j, 0, 0)),
        ],
        out_specs=pl.BlockSpec((bt, ts, LANES), lambda j, b, s: (b, s, j)),
        out_shape=jax.ShapeDtypeStruct((bsz, seq, D_INNER), BF16),
        scratch_shapes=[pltpu.VMEM((S5_T * LANES, S5_T * LANES), BF16),
                        pltpu.VMEM((ts // S5_T * bt, nstate), F32),
                        pltpu.VMEM((ts // S5_T * bt, nstate), F32),
                        pltpu.VMEM((bt, nstate), F32)],
        compiler_params=_cparams(("arbitrary", "arbitrary", "arbitrary")),
        name="s5_slab",
    )(proj, bd, m_slab, n_slab, a_slab)


def s5_slab_params(lam_re, lam_im, log_step, b_re, b_im, c_re, c_im):
    T, G, P, H = S5_T, S5_GROUPS, S5_STATE, S5_GROUP
    gps = LANES // H
    nslab = G // gps
    hp = lax.Precision.HIGHEST
    step = jnp.exp(log_step.astype(F32))[:, None]
    lam_re = lam_re.astype(F32)
    lam_im = lam_im.astype(F32)
    mag = jnp.exp(lam_re * step)
    lb_re = mag * jnp.cos(lam_im * step)
    lb_im = mag * jnp.sin(lam_im * step)
    den = lam_re * lam_re + lam_im * lam_im
    nr = lb_re - 1.0
    co_re = (nr * lam_re + lb_im * lam_im) / den
    co_im = (lb_im * lam_re - nr * lam_im) / den
    b_re = b_re.astype(F32)
    b_im = b_im.astype(F32)
    bb_re = co_re[..., None] * b_re - co_im[..., None] * b_im
    bb_im = co_re[..., None] * b_im + co_im[..., None] * b_re
    c_re = c_re.astype(F32)
    c_im = c_im.astype(F32)
    j = jnp.arange(T + 1, dtype=F32)[:, None, None]
    pmag = jnp.exp(j * (lam_re * step))
    pw_re = pmag * jnp.cos(j * (lam_im * step))
    pw_im = pmag * jnp.sin(j * (lam_im * step))
    cl_re = c_re[None] * pw_re[:, :, None, :] - c_im[None] * pw_im[:, :, None, :]
    cl_im = c_re[None] * pw_im[:, :, None, :] + c_im[None] * pw_re[:, :, None, :]
    kj = (jnp.einsum('jgkp,gph->jgkh', cl_re, bb_re, precision=hp)
          - jnp.einsum('jgkp,gph->jgkh', cl_im, bb_im, precision=hp))
    eye = jnp.eye(gps, dtype=F32)
    bd = jnp.einsum('djgkh,gq->jdghqk', kj[:T].reshape(T, nslab, gps, H, H), eye).reshape(nslab, T, LANES, LANES)
    rev = T - 1 - jnp.arange(T)
    m_re = pw_re[rev][..., None] * bb_re[None] - pw_im[rev][..., None] * bb_im[None]
    m_im = pw_re[rev][..., None] * bb_im[None] + pw_im[rev][..., None] * bb_re[None]

    def m_blk(m):
        return jnp.einsum('sjgph,gq->jsghqp', m.reshape(T, nslab, gps, P, H), eye).reshape(nslab, T * LANES, gps * P)

    m_slab = jnp.concatenate([m_blk(m_re), m_blk(m_im)], axis=-1)

    def n_blk(cl):
        return jnp.einsum('tjgkp,gq->jqptgk', cl[1:].reshape(T, nslab, gps, H, P), eye).reshape(nslab, gps * P, T * LANES)

    n_slab = jnp.concatenate([n_blk(cl_re), -n_blk(cl_im)], axis=1)
    a = jnp.concatenate([pw_re[T].reshape(nslab, gps * P), pw_im[T].reshape(nslab, gps * P)], axis=-1)
    a_slab = jnp.broadcast_to(a[:, None, :], (nslab, S5_BT, 2 * gps * P))
    return bd.astype(BF16), m_slab.astype(BF16), n_slab.astype(BF16), a_slab


def _s5_post_body(y_ref, u_ref, z_ref, d_ref, w_ref, o_ref):
    yv = y_ref[0].astype(F32) + d_ref[...] * u_ref[0].astype(F32)
    g = 0.5 * yv * (1.0 + jnp.tanh(math.sqrt(2.0 / math.pi) * (yv + 0.044715 * (yv * yv * yv))))
    gate = _sigmoid(jnp.dot(g.astype(BF16), w_ref[...], preferred_element_type=F32))
    o_ref[0] = (g * gate * _silu(z_ref[0].astype(F32))).astype(o_ref.dtype)


def s5_post(y, proj, d_skip, w_glu_bf16):
    bsz, seq, _ = y.shape
    tm = min(OUT_TM, seq)
    return pl.pallas_call(
        _s5_post_body,
        grid=(bsz, seq // tm),
        in_specs=[
            pl.BlockSpec((1, tm, D_INNER), lambda b, s: (b, s, 0)),
            pl.BlockSpec((1, tm, D_INNER), lambda b, s: (b, s, 0)),
            pl.BlockSpec((1, tm, D_INNER), lambda b, s: (b, s, 1)),
            pl.BlockSpec((1, D_INNER), lambda b, s: (0, 0)),
            pl.BlockSpec((D_INNER, D_INNER), lambda b, s: (0, 0)),
        ],
        out_specs=pl.BlockSpec((1, tm, D_INNER), lambda b, s: (b, s, 0)),
        out_shape=jax.ShapeDtypeStruct((bsz, seq, D_INNER), BF16),
        compiler_params=_cparams(("parallel", "parallel")),
        name="s5_post",
    )(y, proj, proj, d_skip.reshape(1, D_INNER).astype(F32), w_glu_bf16)


def _gla_body(q_ref, k_ref, v_ref, z_ref, gk_ref, w2_ref, b2_ref, ng_ref, o_ref, st_ref, att_ref, *, ts):
    C, R = GLA_CHUNK, GLA_SUB
    nsub = C // R
    nt = (((1,), (1,)), ((), ()))

    @pl.when(pl.program_id(2) == 0)
    def _():
        st_ref[...] = jnp.zeros_like(st_ref)

    tri3 = _tri3(C)
    row_r = lax.broadcasted_iota(jnp.int32, (R, C), 0)
    col_r = lax.broadcasted_iota(jnp.int32, (R, C), 1)

    for c in range(ts // C):
        rs = slice(c * C, (c + 1) * C)
        x = jnp.dot(gk_ref[0, rs, :].astype(BF16), w2_ref[...], preferred_element_type=F32) + b2_ref[...]
        g = (jnp.minimum(x, 0.0) - jnp.log(1.0 + jnp.exp(-jnp.abs(x)))) * (1.0 / GLA_TAU)
        b = _cumsum_rows(tri3, g)
        q = q_ref[0, rs, :].astype(F32) * (GLA_DK ** -0.5)
        k = k_ref[0, rs, :].astype(F32)
        v = v_ref[0, rs, :]
        state = st_ref[...]
        o = jnp.dot((q * jnp.exp(b)).astype(BF16), state.astype(BF16), preferred_element_type=F32)

        sub_decay = jnp.max(-jnp.sum(g.reshape(nsub, R, GLA_DK), axis=1))

        def scores_matmul(i, cap):
            bi = b[i * R:(i + 1) * R, :]
            ref = b[i * R - 1:i * R, :] if i > 0 else jnp.zeros((1, GLA_DK), F32)
            qs = q[i * R:(i + 1) * R, :] * jnp.exp(bi - ref)
            ks = k * jnp.exp(jnp.minimum(ref - b, cap))
            return lax.dot_general(qs.astype(BF16), ks.astype(BF16), nt, preferred_element_type=F32)

        @pl.when(sub_decay <= GLA_SAFE_DECAY)
        def _():
            for i in range(nsub):
                att_ref[i * R:(i + 1) * R, :] = jnp.where(col_r <= row_r + i * R, scores_matmul(i, GLA_SAFE_DECAY), 0.0)

        @pl.when(sub_decay > GLA_SAFE_DECAY)
        def _():
            for i in range(nsub):
                bi = b[i * R:(i + 1) * R, :]
                qi = q[i * R:(i + 1) * R, :]
                ki = k[i * R:(i + 1) * R, :]
                dg = jnp.zeros((R, C), F32)
                for s in range(R):
                    w = jnp.exp(jnp.minimum(bi - bi[s:s + 1, :], 0.0))
                    cs = jnp.sum(qi * ki[s:s + 1, :] * w, axis=-1, keepdims=True)
                    dg = jnp.where((col_r == i * R + s) & (row_r >= s), cs, dg)
                if i > 0:
                    dg = dg + jnp.where(col_r < i * R, scores_matmul(i, 0.0), 0.0)
                att_ref[i * R:(i + 1) * R, :] = dg

        o = o + jnp.dot(att_ref[...].astype(BF16), v, preferred_element_type=F32)

        b_last = b[C - 1:C, :]
        kd = k * jnp.exp(b_last - b)
        dec = jnp.broadcast_to(jnp.exp(b_last), (C, GLA_DK)).T
        upd = jnp.dot(kd.T.astype(BF16), v, preferred_element_type=F32)
        st_ref[...] = state * jnp.concatenate([dec] * (GLA_DV // GLA_DK), axis=1) + upd

        on = o * lax.rsqrt(jnp.mean(o * o, axis=-1, keepdims=True) + EPS) * ng_ref[...]
        o_ref[0, rs, :] = (on * _silu(z_ref[0, rs, :].astype(F32))).astype(o_ref.dtype)


def gla_mixer(proj, gk_aux, w_gk2, b_gk2, norm_g):
    bsz, seq, _ = proj.shape
    ts = min(GLA_TS, seq)
    w2 = jnp.zeros((LANES, GLA_HEADS * GLA_DK), F32).at[:GLA_RANK].set(w_gk2.astype(F32)).astype(BF16)
    return pl.pallas_call(
        functools.partial(_gla_body, ts=ts),
        grid=(bsz, GLA_HEADS, seq // ts),
        in_specs=[
            pl.BlockSpec((1, ts, GLA_DK), lambda b, h, s: (b, s, h)),
            pl.BlockSpec((1, ts, GLA_DK), lambda b, h, s: (b, s, GLA_HEADS + h)),
            pl.BlockSpec((1, ts, GLA_DV), lambda b, h, s: (b, s, 2 + h)),
            pl.BlockSpec((1, ts, GLA_DV), lambda b, h, s: (b, s, 6 + h)),
            pl.BlockSpec((1, ts, LANES), lambda b, h, s: (b, s, 0)),
            pl.BlockSpec((LANES, GLA_DK), lambda b, h, s: (0, h)),
            pl.BlockSpec((1, GLA_DK), lambda b, h, s: (0, h)),
            pl.BlockSpec((1, GLA_DV), lambda b, h, s: (0, 0)),
        ],
        out_specs=pl.BlockSpec((1, ts, GLA_DV), lambda b, h, s: (b, s, h)),
        out_shape=jax.ShapeDtypeStruct((bsz, seq, D_INNER), BF16),
        scratch_shapes=[pltpu.VMEM((GLA_DK, GLA_DV), F32), pltpu.VMEM((GLA_CHUNK, GLA_CHUNK), F32)],
        compiler_params=_cparams(("parallel", "parallel", "arbitrary")),
        name="gla_mixer",
    )(proj, proj, proj, proj, gk_aux, w2, b_gk2.reshape(1, -1).astype(F32), norm_g.reshape(1, -1).astype(F32))


def _diff_prep_body(x_ref, g_ref, cos_ref, sin_ref, o_ref):
    lane = lax.broadcasted_iota(jnp.int32, (1, LANES), 1)
    first = (lane % DIFF_HALF) < (DIFF_HALF // 2)
    gr = lax.broadcasted_iota(jnp.int32, (LANES, LANES), 0) // DIFF_HALF
    gc = lax.broadcasted_iota(jnp.int32, (LANES, LANES), 1) // DIFF_HALF
    gmat = jnp.where(gr == gc, 1.0 / DIFF_HALF, 0.0).astype(BF16)
    cos = cos_ref[0]
    sin = jnp.where(first, -sin_ref[0], sin_ref[0])
    g = g_ref[0]
    for c in range(D_INNER // LANES):
        x = x_ref[0, :, c * LANES:(c + 1) * LANES].astype(F32)
        x2 = x * x
        x2h = x2.astype(BF16)
        x2l = (x2 - x2h.astype(F32)).astype(BF16)
        ms = jnp.dot(x2h, gmat, preferred_element_type=F32) + jnp.dot(x2l, gmat, preferred_element_type=F32)
        xn = x * lax.rsqrt(ms + EPS) * g
        partner = jnp.where(first, pltpu.roll(xn, LANES - DIFF_HALF // 2, 1), pltpu.roll(xn, DIFF_HALF // 2, 1))
        o_ref[0, 0, :, c * LANES:(c + 1) * LANES] = (xn * cos + partner * sin).astype(o_ref.dtype)


def diff_prep(proj, gains, cos_t, sin_t):
    bsz, seq, _ = proj.shape
    tm = min(OUT_TM, seq)
    return pl.pallas_call(
        _diff_prep_body,
        grid=(2, bsz, seq // tm),
        in_specs=[
            pl.BlockSpec((1, tm, D_INNER), lambda w, b, s: (b, s, w)),
            pl.BlockSpec((1, 1, LANES), lambda w, b, s: (w, 0, 0)),
            pl.BlockSpec((1, tm, LANES), lambda w, b, s: (b, s, 0)),
            pl.BlockSpec((1, tm, LANES), lambda w, b, s: (b, s, 0)),
        ],
        out_specs=pl.BlockSpec((1, 1, tm, D_INNER), lambda w, b, s: (w, b, s, 0)),
        out_shape=jax.ShapeDtypeStruct((2, bsz, seq, D_INNER), BF16),
        compiler_params=_cparams(("parallel", "parallel", "parallel")),
        name="diff_prep",
    )(proj, gains, cos_t, sin_t)


def _diff_epilogue(on, lam, sg, z, scale_out):
    r = on.shape[0] // 2
    o = on[:r] - lam * on[r:]
    o = o * lax.rsqrt(jnp.mean(o * o, axis=-1, keepdims=True) + EPS) * sg * scale_out
    return o * _silu(z.astype(F32))


def _diff_attn_online_body(q_ref, k_ref, v_ref, z_ref, lam_ref, sg_ref, o_ref, *, tq, scale_out):
    qi = pl.program_id(2)
    q = q_ref[0, 0]
    lane = lax.broadcasted_iota(jnp.int32, (tq, LANES), 1)
    zero = jnp.zeros_like(q)
    qq = jnp.concatenate([jnp.where(lane < DIFF_HALF, q, zero), jnp.where(lane >= DIFF_HALF, q, zero)], axis=0)

    def block(j, carry, masked):
        m, l, acc = carry
        k0 = pl.multiple_of(j * tq, tq)
        kb = k_ref[0, 0, pl.ds(k0, tq), :]
        vb = v_ref[0, pl.ds(k0, tq), :]
        s = lax.dot_general(qq, kb, (((1,), (1,)), ((), ())), preferred_element_type=F32)
        if masked:
            r = lax.broadcasted_iota(jnp.int32, (2 * tq, tq), 0)
            r = jnp.where(r >= tq, r - tq, r)
            cidx = lax.broadcasted_iota(jnp.int32, (2 * tq, tq), 1)
            s = jnp.where(cidx <= r, s, -jnp.inf)
        m_new = jnp.maximum(m, jnp.max(s, axis=-1, keepdims=True))
        alpha = jnp.exp2(m - m_new)
        p = jnp.exp2(s - m_new)
        l = alpha * l + jnp.sum(p, axis=-1, keepdims=True)
        acc = alpha * acc + jnp.dot(p.astype(BF16), vb, preferred_element_type=F32)
        return m_new, l, acc

    init = (jnp.full((2 * tq, 1), -jnp.inf, F32), jnp.zeros((2 * tq, 1), F32), jnp.zeros((2 * tq, LANES), F32))
    carry = lax.fori_loop(0, qi, lambda j, cr: block(j, cr, False), init)
    m, l, acc = block(qi, carry, True)
    o_ref[0] = _diff_epilogue(acc / l, lam_ref[...], sg_ref[...], z_ref[0], scale_out).astype(o_ref.dtype)


def _diff_attn_direct_body(q_ref, k_ref, v_ref, z_ref, lam_ref, sg_ref, o_ref, *, seq, scale_out):
    tq, hq = DIFF_TQ, DIFF_TQ // 2
    lane = lax.broadcasted_iota(jnp.int32, (tq, LANES), 1)
    tri = (lax.broadcasted_iota(jnp.int32, (2 * hq, hq), 1)
           <= lax.broadcasted_iota(jnp.int32, (2 * hq, hq), 0) % hq)
    nt = (((1,), (1,)), ((), ()))

    def lanesum(p):
        acc = p[:, :LANES]
        for c in range(1, p.shape[1] // LANES):
            acc = acc + p[:, c * LANES:(c + 1) * LANES]
        return acc

    def qtile(i):
        r0 = i * tq
        q = q_ref[0, 0, pl.ds(r0, tq), :]
        zero = jnp.zeros_like(q)
        q0 = jnp.where(lane < DIFF_HALF, q, zero)
        q1 = jnp.where(lane >= DIFF_HALF, q, zero)
        qq = jnp.concatenate([q0[:hq], q1[:hq], q0[hq:], q1[hq:]], axis=0)

        def offdiag(j, carry):
            acc, l = carry
            c0 = j * tq
            p = jnp.exp2(lax.dot_general(qq, k_ref[0, 0, pl.ds(c0, tq), :], nt, preferred_element_type=F32))
            return (acc + jnp.dot(p.astype(BF16), v_ref[0, pl.ds(c0, tq), :], preferred_element_type=F32),
                    l + lanesum(p))

        carry = (jnp.zeros((2 * tq, LANES), F32), jnp.zeros((2 * tq, LANES), F32))
        for j in range(i):
            carry = offdiag(j, carry)
        acc, l = carry
        r1 = r0 + hq
        k_a, v_a = k_ref[0, 0, pl.ds(r0, hq), :], v_ref[0, pl.ds(r0, hq), :]
        k_b, v_b = k_ref[0, 0, pl.ds(r1, hq), :], v_ref[0, pl.ds(r1, hq), :]
        qq_a, qq_b = qq[:tq], qq[tq:]
        p_aa = jnp.where(tri, jnp.exp2(lax.dot_general(qq_a, k_a, nt, preferred_element_type=F32)), 0.0)
        p_ba = jnp.exp2(lax.dot_general(qq_b, k_a, nt, preferred_element_type=F32))
        p_bb = jnp.where(tri, jnp.exp2(lax.dot_general(qq_b, k_b, nt, preferred_element_type=F32)), 0.0)
        acc_a = acc[:tq] + jnp.dot(p_aa.astype(BF16), v_a, preferred_element_type=F32)
        acc_b = (acc[tq:] + jnp.dot(p_ba.astype(BF16), v_a, preferred_element_type=F32)
                 + jnp.dot(p_bb.astype(BF16), v_b, preferred_element_type=F32))
        l_a = jnp.sum(l[:tq] + lanesum(p_aa), axis=-1, keepdims=True)
        l_b = jnp.sum(l[tq:] + lanesum(p_ba) + lanesum(p_bb), axis=-1, keepdims=True)
        o_ref[0, pl.ds(r0, hq), :] = _diff_epilogue(
            acc_a / l_a, lam_ref[...], sg_ref[...], z_ref[0, pl.ds(r0, hq), :], scale_out).astype(o_ref.dtype)
        o_ref[0, pl.ds(r1, hq), :] = _diff_epilogue(
            acc_b / l_b, lam_ref[...], sg_ref[...], z_ref[0, pl.ds(r1, hq), :], scale_out).astype(o_ref.dtype)

    for i in range(seq // tq):
        qtile(i)


def diff_attn(qk, proj, lam_vec, subln_g, lam_init, score_bound_log2):
    _, bsz, seq, _ = qk.shape
    voff = 2 * D_INNER // LANES
    zoff = 3 * D_INNER // LANES
    sg = subln_g.reshape(1, LANES).astype(F32)
    vec = pl.BlockSpec((1, LANES), lambda *_: (0, 0))

    def online(qk, proj, lam_vec, sg):
        tq = min(DIFF_TQ, seq)
        return pl.pallas_call(
            functools.partial(_diff_attn_online_body, tq=tq, scale_out=1.0 - lam_init),
            grid=(bsz, DIFF_HEADS, seq // tq),
            in_specs=[
                pl.BlockSpec((1, 1, tq, LANES), lambda b, h, i: (0, b, i, h)),
                pl.BlockSpec((1, 1, seq, LANES), lambda b, h, i: (1, b, 0, h)),
                pl.BlockSpec((1, seq, LANES), lambda b, h, i: (b, 0, voff + h)),
                pl.BlockSpec((1, tq, LANES), lambda b, h, i: (b, i, zoff + h)),
                vec, vec,
            ],
            out_specs=pl.BlockSpec((1, tq, LANES), lambda b, h, i: (b, i, h)),
            out_shape=jax.ShapeDtypeStruct((bsz, seq, D_INNER), BF16),
            compiler_params=_cparams(("parallel", "parallel", "arbitrary")),
            name="diff_attn_online",
        )(qk, qk, proj, proj, lam_vec, sg)

    def direct(qk, proj, lam_vec, sg):
        return pl.pallas_call(
            functools.partial(_diff_attn_direct_body, seq=seq, scale_out=1.0 - lam_init),
            grid=(bsz, DIFF_HEADS),
            in_specs=[
                pl.BlockSpec((1, 1, seq, LANES), lambda b, h: (0, b, 0, h)),
                pl.BlockSpec((1, 1, seq, LANES), lambda b, h: (1, b, 0, h)),
                pl.BlockSpec((1, seq, LANES), lambda b, h: (b, 0, voff + h)),
                pl.BlockSpec((1, seq, LANES), lambda b, h: (b, 0, zoff + h)),
                vec, vec,
            ],
            out_specs=pl.BlockSpec((1, seq, LANES), lambda b, h: (b, 0, h)),
            out_shape=jax.ShapeDtypeStruct((bsz, seq, D_INNER), BF16),
            compiler_params=_cparams(("parallel", "parallel")),
            name="diff_attn_direct",
        )(qk, qk, proj, proj, lam_vec, sg)

    if seq % DIFF_TQ != 0:
        return online(qk, proj, lam_vec, sg)
    return lax.cond(score_bound_log2 <= DIFF_DIRECT_MAX_LOG2, direct, online, qk, proj, lam_vec, sg)


def _ssd_body(xbc_ref, z_ref, dt_ref, cw_ref, cb_ref, dtb_ref, alog_ref, dsk_ref, ng_ref, e64_ref, e512_ref,
              o_ref, xs_ref, st_ref, *, ts):
    L = SSD_CHUNK
    gw = SSD_HPG * SSD_HEAD_DIM
    pad = 8

    @pl.when(pl.program_id(1) == 0)
    def _():
        st_ref[...] = jnp.zeros_like(st_ref)
        xs_ref[0:pad, :] = jnp.zeros((pad, SSD_CONV_CH), F32)

    @pl.when(pl.program_id(1) > 0)
    def _():
        xs_ref[0:pad, :] = xs_ref[ts:ts + pad, :]

    xs_ref[pad:pad + ts, :] = xbc_ref[0].astype(F32)
    cw = cw_ref[...]
    acc = cb_ref[...] + xs_ref[pad:pad + ts, :] * cw[SSD_CONV - 1:SSD_CONV, :]
    for w in range(SSD_CONV - 1):
        sh = SSD_CONV - 1 - w
        acc = acc + xs_ref[pad - sh:pad - sh + ts, :] * cw[w:w + 1, :]
    xc_all = _silu(acc)

    tri3 = _tri3(L)
    lane_grp = lax.broadcasted_iota(jnp.int32, (L, LANES), 1) // SSD_HEADS
    row4 = lax.broadcasted_iota(jnp.int32, (L, SSD_HPG * L), 0)
    col4 = lax.broadcasted_iota(jnp.int32, (L, SSD_HPG * L), 1) % L
    eye4 = (row4 == col4).astype(F32)
    causal4 = col4 <= row4
    br = lax.broadcasted_iota(jnp.int32, (SSD_HPG * L, gw), 0) // L
    bc = lax.broadcasted_iota(jnp.int32, (SSD_HPG * L, gw), 1) // SSD_HEAD_DIM
    blockdiag = br == bc
    a_neg = -jnp.exp(alog_ref[...])

    for c in range(ts // L):
        rs = slice(c * L, (c + 1) * L)
        xc = xc_all[rs, :]
        xr = dt_ref[0, rs, :] + dtb_ref[...]
        dt = jnp.maximum(xr, 0.0) + jnp.log(1.0 + jnp.exp(-jnp.abs(xr)))
        cum = _cumsum_rows(tri3, dt * a_neg)
        cum_p = _pack_split3(cum, lane_grp)
        cum64 = jnp.dot(cum_p, e64_ref[...], preferred_element_type=F32)
        dt64 = jnp.dot(_pack_split3(dt, lane_grp), e64_ref[...], preferred_element_type=F32)
        cum512 = jnp.dot(cum_p, e512_ref[...], preferred_element_type=F32)
        last64 = cum64[L - 1:L, :]
        ys = []
        for g in range(SSD_GROUPS):
            xg = xc[:, g * gw:(g + 1) * gw]
            bg = xc[:, D_INNER + g * SSD_STATE:D_INNER + (g + 1) * SSD_STATE]
            cg = xc[:, D_INNER + SSD_GROUPS * SSD_STATE + g * SSD_STATE:
                    D_INNER + SSD_GROUPS * SSD_STATE + (g + 1) * SSD_STATE]
            cgb = cg.astype(BF16)
            cb = lax.dot_general(cgb, bg.astype(BF16), (((1,), (1,)), ((), ())), preferred_element_type=F32)
            ce = cum512[:, g * SSD_HPG * L:(g + 1) * SSD_HPG * L]
            crow = jnp.sum(ce * eye4, axis=0, keepdims=True)
            m = jnp.where(causal4, jnp.exp(jnp.minimum(ce - crow, 0.0)), 0.0) * jnp.concatenate([cb] * SSD_HPG, axis=1)
            cg64 = cum64[:, g * gw:(g + 1) * gw]
            dg64 = dt64[:, g * gw:(g + 1) * gw]
            xdt = xg * dg64
            rhs = jnp.where(blockdiag, jnp.concatenate([xdt] * SSD_HPG, axis=0), 0.0)
            y = jnp.dot(m.astype(BF16), rhs.astype(BF16), preferred_element_type=F32)
            state = st_ref[g]
            y = y + jnp.dot(cgb, state.astype(BF16), preferred_element_type=F32) * jnp.exp(cg64)
            lg = last64[:, g * gw:(g + 1) * gw]
            xw = xg * (jnp.exp(lg - cg64) * dg64)
            st_ref[g] = state * jnp.exp(lg) + jnp.dot(bg.T.astype(BF16), xw.astype(BF16), preferred_element_type=F32)
            ys.append(y + dsk_ref[:, g * gw:(g + 1) * gw] * xg)
        y = jnp.concatenate(ys, axis=1) * _silu(z_ref[0, rs, :].astype(F32))
        y = y * lax.rsqrt(jnp.mean(y * y, axis=-1, keepdims=True) + EPS) * ng_ref[...]
        o_ref[0, rs, :] = y.astype(o_ref.dtype)


def _ssd_expanders():
    e64 = np.zeros((LANES, D_INNER), np.float32)
    e512 = np.zeros((LANES, SSD_HEADS * SSD_CHUNK), np.float32)
    for rep in range(SSD_REPS):
        for h in range(SSD_HEADS):
            e64[rep * SSD_HEADS + h, h * SSD_HEAD_DIM:(h + 1) * SSD_HEAD_DIM] = 1.0
            e512[rep * SSD_HEADS + h, h * SSD_CHUNK:(h + 1) * SSD_CHUNK] = 1.0
    return jnp.asarray(e64, BF16), jnp.asarray(e512, BF16)


def ssd_mixer(proj, dt_aux, conv_w, conv_b, dt_bias, a_log, d_skip, norm_g):
    bsz, seq, _ = proj.shape
    ts = min(SSD_TS, seq)
    e64, e512 = _ssd_expanders()

    def pad128(v):
        return jnp.zeros((1, LANES), F32).at[0, :SSD_REPS * SSD_HEADS].set(jnp.tile(v.astype(F32), SSD_REPS))

    dsk = jnp.repeat(d_skip.astype(F32), SSD_HEAD_DIM).reshape(1, D_INNER)
    full = lambda shape: pl.BlockSpec(shape, lambda b, s: (0,) * len(shape))
    return pl.pallas_call(
        functools.partial(_ssd_body, ts=ts),
        grid=(bsz, seq // ts),
        in_specs=[
            pl.BlockSpec((1, ts, SSD_CONV_CH), lambda b, s: (b, s, 0)),
            pl.BlockSpec((1, ts, D_INNER), lambda b, s: (b, s, 2)),
            pl.BlockSpec((1, ts, LANES), lambda b, s: (b, s, 0)),
            full((SSD_CONV, SSD_CONV_CH)),
            full((1, SSD_CONV_CH)),
            full((1, LANES)),
            full((1, LANES)),
            full((1, D_INNER)),
            full((1, D_INNER)),
            full((LANES, D_INNER)),
            full((LANES, SSD_HEADS * SSD_CHUNK)),
        ],
        out_specs=pl.BlockSpec((1, ts, D_INNER), lambda b, s: (b, s, 0)),
        out_shape=jax.ShapeDtypeStruct((bsz, seq, D_INNER), BF16),
        scratch_shapes=[pltpu.VMEM((ts + 16, SSD_CONV_CH), F32),
                        pltpu.VMEM((SSD_GROUPS, SSD_STATE, SSD_HPG * SSD_HEAD_DIM), F32)],
        compiler_params=_cparams(("parallel", "arbitrary")),
        name="ssd_mixer",
    )(proj, proj, dt_aux, conv_w.astype(F32), conv_b.reshape(1, -1).astype(F32), pad128(dt_bias), pad128(a_log),
      dsk, norm_g.reshape(1, -1).astype(F32), e64, e512)


def _pad_cols(w, width=LANES):
    return jnp.zeros((w.shape[0], width), F32).at[:, :w.shape[1]].set(w.astype(F32))


def s5_layer(x, norm_g, w_in, lam_re, lam_im, log_step, b_re, b_im, c_re, c_im, d_skip, w_glu):
    bsz, seq, d = x.shape
    proj = in_proj(x.reshape(bsz * seq, d), norm_g, w_in.astype(BF16)).reshape(bsz, seq, -1)
    y = s5_slab(proj, *s5_slab_params(lam_re, lam_im, log_step, b_re, b_im, c_re, c_im))
    branch = s5_post(y, proj, d_skip, w_glu.astype(BF16))
    return branch, proj, (2 * D_INNER) // D_X, (2 * D_INNER) // D_X + 1


def gla_layer(x, norm_g, w_in, w_gk2, b_gk2, gla_norm_g):
    bsz, seq, d = x.shape
    nqk = GLA_HEADS * GLA_DK
    c0 = 2 * nqk + D_INNER
    w_main = jnp.concatenate([w_in[:, :c0], w_in[:, c0 + GLA_RANK:]], axis=1).astype(BF16)
    proj, gk_aux = in_proj(x.reshape(bsz * seq, d), norm_g, w_main, _pad_cols(w_in[:, c0:c0 + GLA_RANK]))
    proj = proj.reshape(bsz, seq, -1)
    branch = gla_mixer(proj, gk_aux.reshape(bsz, seq, LANES), w_gk2, b_gk2, gla_norm_g)
    zx_col = (c0 + D_INNER) // D_X
    return branch, proj, zx_col, zx_col + 1


def diff_layer(x, norm_g, w_in, cos_t, sin_t, q_g, k_g, lq1, lk1, lq2, lk2, subln_g, lam_init):
    bsz, seq, d = x.shape
    proj = in_proj(x.reshape(bsz * seq, d), norm_g, w_in.astype(BF16)).reshape(bsz, seq, -1)
    q_scale = (DIFF_HALF ** -0.5) * math.log2(math.e)
    gains = jnp.stack([jnp.tile(q_g.astype(F32), LANES // DIFF_HALF) * q_scale,
                       jnp.tile(k_g.astype(F32), LANES // DIFF_HALF)]).reshape(2, 1, LANES)
    qk = diff_prep(proj, gains, cos_t, sin_t)
    score_bound_log2 = (BF16_ROUNDING_SLACK * DIFF_HALF * q_scale
                        * jnp.max(jnp.abs(q_g.astype(F32))) * jnp.max(jnp.abs(k_g.astype(F32))))
    lam = (jnp.exp(jnp.sum(lq1.astype(F32) * lk1.astype(F32)))
           - jnp.exp(jnp.sum(lq2.astype(F32) * lk2.astype(F32))) + lam_init)
    lam_vec = jnp.full((1, LANES), lam, F32)
    branch = diff_attn(qk, proj, lam_vec, subln_g, lam_init, score_bound_log2)
    zx_col = (4 * D_INNER) // D_X
    return branch, proj, zx_col, zx_col + 1


def ssd_layer(x, norm_g, w_in, conv_w, conv_b, dt_bias, a_log, d_skip, ssd_norm_g):
    bsz, seq, d = x.shape
    c0 = SSD_CONV_CH
    w_main = jnp.concatenate([w_in[:, :c0], w_in[:, c0 + SSD_HEADS:]], axis=1).astype(BF16)
    w_dt = jnp.tile(w_in[:, c0:c0 + SSD_HEADS], (1, SSD_REPS))
    proj, dt_aux = in_proj(x.reshape(bsz * seq, d), norm_g, w_main, _pad_cols(w_dt))
    proj = proj.reshape(bsz, seq, -1)
    branch = ssd_mixer(proj, dt_aux.reshape(bsz, seq, LANES), conv_w, conv_b, dt_bias, a_log, d_skip, ssd_norm_g)
    zx_col = (c0 + D_INNER) // D_X
    return branch, proj, zx_col, zx_col + 1


def rope_tables(positions):
    inv = ROPE_THETA ** (-jnp.arange(0, DIFF_HALF, 2, dtype=F32) / DIFF_HALF)
    ang = positions.astype(F32)[..., None] * inv
    reps = LANES // (DIFF_HALF // 2)
    return jnp.tile(jnp.cos(ang), (1, 1, reps)), jnp.tile(jnp.sin(ang), (1, 1, reps))


def kernel(x, mem, positions, norm_g, w_out, mem_norm_g, w_mem_kv, xq_g, xk_g, s5_w_in, s5_lam_re, s5_lam_im, s5_log_step, s5_b_re, s5_b_im, s5_c_re, s5_c_im, s5_d, s5_w_glu, gla_w_in, gla_w_gk2, gla_b_gk2, gla_norm_g, diff_w_in, diff_q_g, diff_k_g, diff_lq1, diff_lk1, diff_lq2, diff_lk2, diff_subln_g, ssd_w_in, ssd_conv_w, ssd_conv_b, ssd_dt_bias, ssd_a_log, ssd_d, ssd_norm_g):
    depth = norm_g.shape[0]
    k_all, v_all = mem_kv(mem, mem_norm_g, w_mem_kv, xk_g)
    cos_t, sin_t = rope_tables(positions)
    w_out_b = w_out.astype(BF16)
    for i in range(depth):
        kind, j = i % 4, i // 4
        if kind == 0:
            branch, proj, zc, qc = s5_layer(x, norm_g[i], s5_w_in[j], s5_lam_re[j], s5_lam_im[j], s5_log_step[j],
                                            s5_b_re[j], s5_b_im[j], s5_c_re[j], s5_c_im[j], s5_d[j], s5_w_glu[j])
        elif kind == 1:
            branch, proj, zc, qc = gla_layer(x, norm_g[i], gla_w_in[j], gla_w_gk2[j], gla_b_gk2[j], gla_norm_g[j])
        elif kind == 2:
            lam_init = 0.8 - 0.6 * math.exp(-0.3 * i)
            branch, proj, zc, qc = diff_layer(x, norm_g[i], diff_w_in[j], cos_t, sin_t, diff_q_g[j], diff_k_g[j],
                                              diff_lq1[j], diff_lk1[j], diff_lq2[j], diff_lk2[j], diff_subln_g[j],
                                              lam_init)
        else:
            branch, proj, zc, qc = ssd_layer(x, norm_g[i], ssd_w_in[j], ssd_conv_w[j], ssd_conv_b[j], ssd_dt_bias[j],
                                             ssd_a_log[j], ssd_d[j], ssd_norm_g[j])
        x = out_stage(branch, proj, zc, qc, k_all[i], v_all[i], x, w_out_b[i], xq_g[i])
    return x
```

```python
import functools
import math

import numpy as np
import jax
import jax.numpy as jnp
from jax import lax
from jax.experimental import pallas as pl
from jax.experimental.pallas import tpu as pltpu

F32 = jnp.float32
BF16 = jnp.bfloat16

D_MODEL = 1024
D_INNER = 2048
EPS = 1e-6
MEM_LEN = 256
X_HEADS = 4
X_HEAD_DIM = 128
D_X = X_HEADS * X_HEAD_DIM
D_GATE = D_INNER + D_X
S5_GROUP = 16
S5_GROUPS = D_INNER // S5_GROUP
S5_STATE = 64
GLA_HEADS = 4
GLA_DK = 128
GLA_DV = 512
GLA_RANK = 16
GLA_TAU = 16.0
DIFF_HEADS = 16
DIFF_HALF = 64
DIFF_VDIM = 128
ROPE_THETA = 10000.0
SSD_HEAD_DIM = 64
SSD_HEADS = 32
SSD_GROUPS = 8
SSD_HPG = 4
SSD_STATE = 128
SSD_CONV = 4
SSD_CONV_CH = D_INNER + 2 * SSD_GROUPS * SSD_STATE

LANES = 128
MXU_WIDTH = 256
VMEM_LIMIT = 48 * 1024 * 1024

PROJ_TM = 1024
PROJ_TN_MAX = 2560
OUT_TM = 512
S5_BT = 16
S5_T = 16
S5_TS = 512
GLA_TS = 256
GLA_CHUNK = 128
GLA_SUB = 16
GLA_SAFE_DECAY = 40.0
DIFF_TQ = 512
DIFF_DIRECT_MAX_LOG2 = 64.0
BF16_ROUNDING_SLACK = 1.02
SSD_TS = 256
SSD_CHUNK = 128
SSD_REPS = 3


def _cparams(sem):
    return pltpu.CompilerParams(dimension_semantics=sem, vmem_limit_bytes=VMEM_LIMIT)


def _sigmoid(x):
    return 1.0 / (1.0 + jnp.exp(-x))


def _silu(x):
    return x * _sigmoid(x)


def _split3(x):
    x1 = x.astype(BF16)
    r1 = x - x1.astype(F32)
    x2 = r1.astype(BF16)
    x3 = (r1 - x2.astype(F32)).astype(BF16)
    return x1, x2, x3


def _cumsum_rows(tri3, x):
    return jnp.dot(tri3, jnp.concatenate(_split3(x), axis=0), preferred_element_type=F32)


def _pack_split3(x, lane_grp):
    x1, x2, x3 = _split3(x)
    zero = jnp.zeros_like(x1)
    return jnp.where(lane_grp == 0, x1, jnp.where(lane_grp == 1, x2, jnp.where(lane_grp == 2, x3, zero)))


def _tri3(n):
    row = lax.broadcasted_iota(jnp.int32, (n, n), 0)
    col = lax.broadcasted_iota(jnp.int32, (n, n), 1)
    tri = (col <= row).astype(BF16)
    return jnp.concatenate([tri, tri, tri], axis=1)


def _in_proj_body(x_ref, g_ref, w_ref, *rest, has_aux):
    if has_aux:
        waux_ref, o_ref, oaux_ref, h_ref = rest
    else:
        o_ref, h_ref = rest

    @pl.when(pl.program_id(1) == 0)
    def _():
        x = x_ref[...]
        h = x * lax.rsqrt(jnp.mean(x * x, axis=-1, keepdims=True) + EPS) * g_ref[...]
        hb = h.astype(BF16)
        h_ref[...] = hb
        if has_aux:
            oaux_ref[...] = jnp.dot(hb, waux_ref[...], preferred_element_type=F32)

    o_ref[...] = jnp.dot(h_ref[...], w_ref[...], preferred_element_type=F32).astype(o_ref.dtype)


def in_proj(x2d, g, w_bf16, w_aux=None):
    n, d = x2d.shape
    nc = w_bf16.shape[1]
    tm = min(PROJ_TM, n)
    tn = max(t for t in range(MXU_WIDTH, PROJ_TN_MAX + 1, MXU_WIDTH) if nc % t == 0)
    assert n % tm == 0
    has_aux = w_aux is not None
    in_specs = [
        pl.BlockSpec((tm, d), lambda i, j: (i, 0)),
        pl.BlockSpec((1, d), lambda i, j: (0, 0)),
        pl.BlockSpec((d, tn), lambda i, j: (0, j)),
    ]
    args = [x2d, g.reshape(1, d).astype(F32), w_bf16]
    out_shape = [jax.ShapeDtypeStruct((n, nc), BF16)]
    out_specs = [pl.BlockSpec((tm, tn), lambda i, j: (i, j))]
    if has_aux:
        in_specs.append(pl.BlockSpec((d, LANES), lambda i, j: (0, 0)))
        args.append(w_aux.astype(BF16))
        out_shape.append(jax.ShapeDtypeStruct((n, LANES), F32))
        out_specs.append(pl.BlockSpec((tm, LANES), lambda i, j: (i, 0)))
    res = pl.pallas_call(
        functools.partial(_in_proj_body, has_aux=has_aux),
        grid=(n // tm, nc // tn),
        in_specs=in_specs,
        out_specs=out_specs,
        out_shape=out_shape,
        scratch_shapes=[pltpu.VMEM((tm, d), BF16)],
        compiler_params=_cparams(("parallel", "arbitrary")),
        name="in_proj_aux" if has_aux else "in_proj",
    )(*args)
    return (res[0], res[1]) if has_aux else res[0]


def _mem_kv_body(mem_ref, mg_ref, w_ref, kg_ref, k_ref, v_ref):
    m = mem_ref[0]
    mn = m * lax.rsqrt(jnp.mean(m * m, axis=-1, keepdims=True) + EPS) * mg_ref[...]
    kv = jnp.dot(mn.astype(BF16), w_ref[0], preferred_element_type=F32)
    kg = kg_ref[0]
    for h in range(X_HEADS):
        kh = kv[:, h * X_HEAD_DIM:(h + 1) * X_HEAD_DIM]
        kh = kh * lax.rsqrt(jnp.mean(kh * kh, axis=-1, keepdims=True) + EPS) * kg
        k_ref[0, 0, :, h * X_HEAD_DIM:(h + 1) * X_HEAD_DIM] = kh.astype(BF16)
    v_ref[0, 0] = kv[:, D_X:].astype(BF16)


def mem_kv(mem, mem_norm_g, w_mem_kv, xk_g):
    bsz, mlen, d = mem.shape
    depth = w_mem_kv.shape[0]
    return pl.pallas_call(
        _mem_kv_body,
        grid=(depth, bsz),
        in_specs=[
            pl.BlockSpec((1, mlen, d), lambda l, b: (b, 0, 0)),
            pl.BlockSpec((1, d), lambda l, b: (0, 0)),
            pl.BlockSpec((1, d, 2 * D_X), lambda l, b: (l, 0, 0)),
            pl.BlockSpec((1, 1, X_HEAD_DIM), lambda l, b: (l, 0, 0)),
        ],
        out_specs=[
            pl.BlockSpec((1, 1, mlen, D_X), lambda l, b: (l, b, 0, 0)),
            pl.BlockSpec((1, 1, mlen, D_X), lambda l, b: (l, b, 0, 0)),
        ],
        out_shape=[jax.ShapeDtypeStruct((depth, bsz, mlen, D_X), BF16)] * 2,
        compiler_params=_cparams(("parallel", "parallel")),
        name="mem_kv",
    )(mem, mem_norm_g.reshape(1, d).astype(F32), w_mem_kv.astype(BF16),
      xk_g.reshape(depth, 1, X_HEAD_DIM).astype(F32))


def _out_body(br_ref, zx_ref, qx_ref, k_ref, v_ref, x_ref, w_ref, qg_ref, o_ref):
    q = qx_ref[0].astype(F32)
    zx = zx_ref[0].astype(F32)
    qg = qg_ref[...]
    outs = []
    for h in range(X_HEADS):
        sl = slice(h * X_HEAD_DIM, (h + 1) * X_HEAD_DIM)
        qh = q[:, sl]
        qn = qh * lax.rsqrt(jnp.mean(qh * qh, axis=-1, keepdims=True) + EPS) * qg
        s = lax.dot_general(qn.astype(BF16), k_ref[0, :, sl], (((1,), (1,)), ((), ())),
                            preferred_element_type=F32) * (X_HEAD_DIM ** -0.5)
        p = jnp.exp(s - jnp.max(s, axis=-1, keepdims=True))
        l = jnp.sum(p, axis=-1, keepdims=True)
        oh = jnp.dot(p.astype(BF16), v_ref[0, :, sl], preferred_element_type=F32) / l
        outs.append((oh * _silu(zx[:, sl])).astype(BF16))
    mem_out = jnp.concatenate(outs, axis=-1)
    acc = jnp.dot(br_ref[0], w_ref[:D_INNER, :], preferred_element_type=F32)
    acc = acc + jnp.dot(mem_out, w_ref[D_INNER:, :], preferred_element_type=F32)
    o_ref[0] = x_ref[0] + acc


def out_stage(branch, proj, zx_col, qx_col, k_l, v_l, x, w_out_bf16, xq_g):
    bsz, seq, _ = branch.shape
    tm = min(OUT_TM, seq)
    return pl.pallas_call(
        _out_body,
        grid=(bsz, seq // tm),
        in_specs=[
            pl.BlockSpec((1, tm, D_INNER), lambda b, s: (b, s, 0)),
            pl.BlockSpec((1, tm, D_X), lambda b, s: (b, s, zx_col)),
            pl.BlockSpec((1, tm, D_X), lambda b, s: (b, s, qx_col)),
            pl.BlockSpec((1, MEM_LEN, D_X), lambda b, s: (b, 0, 0)),
            pl.BlockSpec((1, MEM_LEN, D_X), lambda b, s: (b, 0, 0)),
            pl.BlockSpec((1, tm, D_MODEL), lambda b, s: (b, s, 0)),
            pl.BlockSpec((D_GATE, D_MODEL), lambda b, s: (0, 0)),
            pl.BlockSpec((1, X_HEAD_DIM), lambda b, s: (0, 0)),
        ],
        out_specs=pl.BlockSpec((1, tm, D_MODEL), lambda b, s: (b, s, 0)),
        out_shape=jax.ShapeDtypeStruct((bsz, seq, D_MODEL), F32),
        compiler_params=_cparams(("parallel", "parallel")),
        name="out_stage",
    )(branch, proj, proj, k_l, v_l, x, w_out_bf16, xq_g.reshape(1, X_HEAD_DIM).astype(F32))


def _s5_slab_body(u_ref, bd_ref, mc_ref, nc_ref, a_ref, y_ref, w_ref, m_ref, n_ref, xs_ref, sp_ref, st_ref, *, nc, bt):
    T = S5_T
    gps = LANES // S5_GROUP
    half = S5_STATE * gps

    @pl.when((pl.program_id(1) == 0) & (pl.program_id(2) == 0))
    def _():
        for s in range(T):
            for t in range(max(s - 1, 0), T):
                blk = bd_ref[0, t - s] if t >= s else jnp.zeros((LANES, LANES), BF16)
                w_ref[s * LANES:(s + 1) * LANES, t * LANES:(t + 1) * LANES] = blk
        mrow = lax.broadcasted_iota(jnp.int32, (T * LANES, half), 0) // S5_GROUP % gps
        mcol = lax.broadcasted_iota(jnp.int32, (T * LANES, half), 1) // S5_STATE
        nrow = lax.broadcasted_iota(jnp.int32, (half, T * LANES), 0) // S5_STATE
        ncol = lax.broadcasted_iota(jnp.int32, (half, T * LANES), 1) // S5_GROUP % gps
        for part in range(2):
            mt = jnp.concatenate([mc_ref[0, part]] * (half // LANES), axis=1)
            m_ref[:, part * half:(part + 1) * half] = jnp.where(mrow == mcol, mt, jnp.zeros_like(mt))
            nt = jnp.concatenate([nc_ref[0, part]] * gps, axis=0)
            n_ref[part * half:(part + 1) * half, :] = jnp.where(nrow == ncol, nt, jnp.zeros_like(nt))

    @pl.when(pl.program_id(2) == 0)
    def _():
        st_ref[...] = jnp.zeros_like(st_ref)

    v4 = pltpu.einshape("bcsl->csbl", u_ref[...].reshape(bt, nc, T, LANES))
    v = jnp.concatenate([v4[:, s].reshape(nc * bt, LANES) for s in range(T)], axis=1)
    xs_ref[...] = jnp.dot(v, m_ref[...], preferred_element_type=F32)
    a = a_ref[0]
    a_re, a_im = a[:, :half], a[:, half:]

    def step(c, carry):
        sr, si = carry
        r0 = pl.multiple_of(c * bt, bt)
        sp_ref[pl.ds(r0, bt), :half] = sr
        sp_ref[pl.ds(r0, bt), half:] = si
        x = xs_ref[pl.ds(r0, bt), :]
        return a_re * sr - a_im * si + x[:, :half], a_re * si + a_im * sr + x[:, half:]

    st = st_ref[...]
    sr, si = lax.fori_loop(0, nc, step, (st[:, :half], st[:, half:]), unroll=4)
    st_ref[:, :half] = sr
    st_ref[:, half:] = si
    y_state = jnp.dot(sp_ref[...].astype(BF16), n_ref[...], preferred_element_type=F32)
    cols = []
    for tp in range(T // 2):
        kk = (2 * tp + 2) * LANES
        cols.append(jnp.dot(v[:, :kk], w_ref[:kk, tp * 2 * LANES:(tp + 1) * 2 * LANES], preferred_element_type=F32))
    y = (y_state + jnp.concatenate(cols, axis=1)).astype(BF16)
    y4 = jnp.stack([y[:, t * LANES:(t + 1) * LANES].reshape(nc, bt, LANES) for t in range(T)], axis=1)
    y_ref[...] = pltpu.einshape("ctbl->bctl", y4).reshape(bt, nc * T, LANES)


def s5_slab(proj, bd, m_compact, n_compact, a_slab):
    bsz, seq, _ = proj.shape
    bt = S5_BT
    ts = min(S5_TS, seq)
    nslab = D_INNER // LANES
    nstate = 2 * S5_STATE * (LANES // S5_GROUP)
    assert bsz % bt == 0 and seq % ts == 0
    return pl.pallas_call(
        functools.partial(_s5_slab_body, nc=ts // S5_T, bt=bt),
        grid=(nslab, bsz // bt, seq // ts),
        in_specs=[
            pl.BlockSpec((bt, ts, LANES), lambda j, b, s: (b, s, j)),
            pl.BlockSpec((1, S5_T, LANES, LANES), lambda j, b, s: (j, 0, 0, 0)),
            pl.BlockSpec((1, 2, S5_T * LANES, LANES), lambda j, b, s: (j, 0, 0, 0)),
            pl.BlockSpec((1, 2, S5_STATE, S5_T * LANES), lambda j, b, s: (j, 0, 0, 0)),
            pl.BlockSpec((1, bt, nstate), lambda j, b, s: (j, 0, 0)),
        ],
        out_specs=pl.BlockSpec((bt, ts, LANES), lambda j, b, s: (b, s, j)),
        out_shape=jax.ShapeDtypeStruct((bsz, seq, D_INNER), BF16),
        scratch_shapes=[pltpu.VMEM((S5_T * LANES, S5_T * LANES), BF16),
                        pltpu.VMEM((S5_T * LANES, nstate), BF16),
                        pltpu.VMEM((nstate, S5_T * LANES), BF16),
                        pltpu.VMEM((ts // S5_T * bt, nstate), F32),
                        pltpu.VMEM((ts // S5_T * bt, nstate), F32),
                        pltpu.VMEM((bt, nstate), F32)],
        compiler_params=_cparams(("arbitrary", "arbitrary", "arbitrary")),
        name="s5_slab",
    )(proj, bd, m_compact, n_compact, a_slab)


def s5_slab_params(lam_re, lam_im, log_step, b_re, b_im, c_re, c_im):
    T, G, P, H = S5_T, S5_GROUPS, S5_STATE, S5_GROUP
    gps = LANES // H
    nslab = G // gps
    hp = lax.Precision.HIGHEST
    step = jnp.exp(log_step.astype(F32))[:, None]
    lam_re = lam_re.astype(F32)
    lam_im = lam_im.astype(F32)
    mag = jnp.exp(lam_re * step)
    lb_re = mag * jnp.cos(lam_im * step)
    lb_im = mag * jnp.sin(lam_im * step)
    den = lam_re * lam_re + lam_im * lam_im
    nr = lb_re - 1.0
    co_re = (nr * lam_re + lb_im * lam_im) / den
    co_im = (lb_im * lam_re - nr * lam_im) / den
    b_re = b_re.astype(F32)
    b_im = b_im.astype(F32)
    bb_re = co_re[..., None] * b_re - co_im[..., None] * b_im
    bb_im = co_re[..., None] * b_im + co_im[..., None] * b_re
    c_re = c_re.astype(F32)
    c_im = c_im.astype(F32)
    j = jnp.arange(T + 1, dtype=F32)[:, None, None]
    pmag = jnp.exp(j * (lam_re * step))
    pw_re = pmag * jnp.cos(j * (lam_im * step))
    pw_im = pmag * jnp.sin(j * (lam_im * step))
    cl_re = c_re[None] * pw_re[:, :, None, :] - c_im[None] * pw_im[:, :, None, :]
    cl_im = c_re[None] * pw_im[:, :, None, :] + c_im[None] * pw_re[:, :, None, :]
    kj = jnp.einsum('gjkp,gph->jgkh', jnp.concatenate([cl_re, -cl_im], axis=-1).transpose(1, 0, 2, 3),
                    jnp.concatenate([bb_re, bb_im], axis=1), precision=hp)
    eye = jnp.eye(gps, dtype=F32)
    bd = jnp.einsum('djgkh,gq->jdghqk', kj[:T].reshape(T, nslab, gps, H, H), eye).reshape(nslab, T, LANES, LANES)
    rev = T - 1 - jnp.arange(T)
    m_re = pw_re[rev][..., None] * bb_re[None] - pw_im[rev][..., None] * bb_im[None]
    m_im = pw_re[rev][..., None] * bb_im[None] + pw_im[rev][..., None] * bb_re[None]

    def m_rows(m):
        r = m.reshape(T, nslab, gps, P, H).transpose(1, 0, 2, 4, 3).reshape(nslab, T * LANES, P)
        return jnp.concatenate([r, r], axis=-1)

    m_compact = jnp.stack([m_rows(m_re), m_rows(m_im)], axis=1)

    def n_rows(cl):
        return cl[1:].reshape(T, nslab, gps, H, P).transpose(1, 4, 0, 2, 3).reshape(nslab, P, T * LANES)

    n_compact = jnp.stack([n_rows(cl_re), -n_rows(cl_im)], axis=1)
    a = jnp.concatenate([pw_re[T].reshape(nslab, gps * P), pw_im[T].reshape(nslab, gps * P)], axis=-1)
    a_slab = jnp.broadcast_to(a[:, None, :], (nslab, S5_BT, 2 * gps * P))
    return bd.astype(BF16), m_compact.astype(BF16), n_compact.astype(BF16), a_slab


def _s5_post_body(y_ref, u_ref, z_ref, d_ref, w_ref, o_ref):
    yv = y_ref[0].astype(F32) + d_ref[...] * u_ref[0].astype(F32)
    g = 0.5 * yv * (1.0 + jnp.tanh(math.sqrt(2.0 / math.pi) * (yv + 0.044715 * (yv * yv * yv))))
    gate = _sigmoid(jnp.dot(g.astype(BF16), w_ref[...], preferred_element_type=F32))
    o_ref[0] = (g * gate * _silu(z_ref[0].astype(F32))).astype(o_ref.dtype)


def s5_post(y, proj, d_skip, w_glu_bf16):
    bsz, seq, _ = y.shape
    tm = min(OUT_TM, seq)
    return pl.pallas_call(
        _s5_post_body,
        grid=(bsz, seq // tm),
        in_specs=[
            pl.BlockSpec((1, tm, D_INNER), lambda b, s: (b, s, 0)),
            pl.BlockSpec((1, tm, D_INNER), lambda b, s: (b, s, 0)),
            pl.BlockSpec((1, tm, D_INNER), lambda b, s: (b, s, 1)),
            pl.BlockSpec((1, D_INNER), lambda b, s: (0, 0)),
            pl.BlockSpec((D_INNER, D_INNER), lambda b, s: (0, 0)),
        ],
        out_specs=pl.BlockSpec((1, tm, D_INNER), lambda b, s: (b, s, 0)),
        out_shape=jax.ShapeDtypeStruct((bsz, seq, D_INNER), BF16),
        compiler_params=_cparams(("parallel", "parallel")),
        name="s5_post",
    )(y, proj, proj, d_skip.reshape(1, D_INNER).astype(F32), w_glu_bf16)


def _gla_body(q_ref, k_ref, v_ref, z_ref, gk_ref, w2_ref, b2_ref, ng_ref, o_ref, st_ref, att_ref, *, ts):
    C, R = GLA_CHUNK, GLA_SUB
    nsub = C // R
    nt = (((1,), (1,)), ((), ()))

    @pl.when(pl.program_id(2) == 0)
    def _():
        st_ref[...] = jnp.zeros_like(st_ref)

    tri3 = _tri3(C)
    row_r = lax.broadcasted_iota(jnp.int32, (R, C), 0)
    col_r = lax.broadcasted_iota(jnp.int32, (R, C), 1)

    for c in range(ts // C):
        rs = slice(c * C, (c + 1) * C)
        x = jnp.dot(gk_ref[0, rs, :].astype(BF16), w2_ref[...], preferred_element_type=F32) + b2_ref[...]
        g = (jnp.minimum(x, 0.0) - jnp.log(1.0 + jnp.exp(-jnp.abs(x)))) * (1.0 / GLA_TAU)
        b = _cumsum_rows(tri3, g)
        q = q_ref[0, rs, :].astype(F32) * (GLA_DK ** -0.5)
        k = k_ref[0, rs, :].astype(F32)
        v = v_ref[0, rs, :]
        state = st_ref[...]
        o = jnp.dot((q * jnp.exp(b)).astype(BF16), state.astype(BF16), preferred_element_type=F32)

        sub_decay = jnp.max(-jnp.sum(g.reshape(nsub, R, GLA_DK), axis=1))

        def scores_matmul(i, cap):
            bi = b[i * R:(i + 1) * R, :]
            ref = b[i * R - 1:i * R, :] if i > 0 else jnp.zeros((1, GLA_DK), F32)
            qs = q[i * R:(i + 1) * R, :] * jnp.exp(bi - ref)
            ks = k * jnp.exp(jnp.minimum(ref - b, cap))
            return lax.dot_general(qs.astype(BF16), ks.astype(BF16), nt, preferred_element_type=F32)

        @pl.when(sub_decay <= GLA_SAFE_DECAY)
        def _():
            for i in range(nsub):
                att_ref[i * R:(i + 1) * R, :] = jnp.where(col_r <= row_r + i * R, scores_matmul(i, GLA_SAFE_DECAY), 0.0)

        @pl.when(sub_decay > GLA_SAFE_DECAY)
        def _():
            for i in range(nsub):
                bi = b[i * R:(i + 1) * R, :]
                qi = q[i * R:(i + 1) * R, :]
                ki = k[i * R:(i + 1) * R, :]
                dg = jnp.zeros((R, C), F32)
                for s in range(R):
                    w = jnp.exp(jnp.minimum(bi - bi[s:s + 1, :], 0.0))
                    cs = jnp.sum(qi * ki[s:s + 1, :] * w, axis=-1, keepdims=True)
                    dg = jnp.where((col_r == i * R + s) & (row_r >= s), cs, dg)
                if i > 0:
                    dg = dg + jnp.where(col_r < i * R, scores_matmul(i, 0.0), 0.0)
                att_ref[i * R:(i + 1) * R, :] = dg

        o = o + jnp.dot(att_ref[...].astype(BF16), v, preferred_element_type=F32)

        b_last = b[C - 1:C, :]
        kd = k * jnp.exp(b_last - b)
        dec = jnp.broadcast_to(jnp.exp(b_last), (C, GLA_DK)).T
        upd = jnp.dot(kd.T.astype(BF16), v, preferred_element_type=F32)
        st_ref[...] = state * jnp.concatenate([dec] * (GLA_DV // GLA_DK), axis=1) + upd

        on = o * lax.rsqrt(jnp.mean(o * o, axis=-1, keepdims=True) + EPS) * ng_ref[...]
        o_ref[0, rs, :] = (on * _silu(z_ref[0, rs, :].astype(F32))).astype(o_ref.dtype)


def gla_mixer(proj, gk_aux, w_gk2, b_gk2, norm_g):
    bsz, seq, _ = proj.shape
    ts = min(GLA_TS, seq)
    w2 = jnp.zeros((LANES, GLA_HEADS * GLA_DK), F32).at[:GLA_RANK].set(w_gk2.astype(F32)).astype(BF16)
    return pl.pallas_call(
        functools.partial(_gla_body, ts=ts),
        grid=(bsz, GLA_HEADS, seq // ts),
        in_specs=[
            pl.BlockSpec((1, ts, GLA_DK), lambda b, h, s: (b, s, h)),
            pl.BlockSpec((1, ts, GLA_DK), lambda b, h, s: (b, s, GLA_HEADS + h)),
            pl.BlockSpec((1, ts, GLA_DV), lambda b, h, s: (b, s, 2 + h)),
            pl.BlockSpec((1, ts, GLA_DV), lambda b, h, s: (b, s, 6 + h)),
            pl.BlockSpec((1, ts, LANES), lambda b, h, s: (b, s, 0)),
            pl.BlockSpec((LANES, GLA_DK), lambda b, h, s: (0, h)),
            pl.BlockSpec((1, GLA_DK), lambda b, h, s: (0, h)),
            pl.BlockSpec((1, GLA_DV), lambda b, h, s: (0, 0)),
        ],
        out_specs=pl.BlockSpec((1, ts, GLA_DV), lambda b, h, s: (b, s, h)),
        out_shape=jax.ShapeDtypeStruct((bsz, seq, D_INNER), BF16),
        scratch_shapes=[pltpu.VMEM((GLA_DK, GLA_DV), F32), pltpu.VMEM((GLA_CHUNK, GLA_CHUNK), F32)],
        compiler_params=_cparams(("parallel", "parallel", "arbitrary")),
        name="gla_mixer",
    )(proj, proj, proj, proj, gk_aux, w2, b_gk2.reshape(1, -1).astype(F32), norm_g.reshape(1, -1).astype(F32))


def _norm_rope(x, g, cos, sin):
    lane = lax.broadcasted_iota(jnp.int32, (1, LANES), 1)
    first = (lane % DIFF_HALF) < (DIFF_HALF // 2)
    gr = lax.broadcasted_iota(jnp.int32, (LANES, LANES), 0) // DIFF_HALF
    gc = lax.broadcasted_iota(jnp.int32, (LANES, LANES), 1) // DIFF_HALF
    gmat = jnp.where(gr == gc, 1.0 / DIFF_HALF, 0.0).astype(BF16)
    x2 = x * x
    x2h = x2.astype(BF16)
    x2l = (x2 - x2h.astype(F32)).astype(BF16)
    ms = jnp.dot(x2h, gmat, preferred_element_type=F32) + jnp.dot(x2l, gmat, preferred_element_type=F32)
    xn = x * lax.rsqrt(ms + EPS) * g
    partner = jnp.where(first, pltpu.roll(xn, LANES - DIFF_HALF // 2, 1), pltpu.roll(xn, DIFF_HALF // 2, 1))
    return xn * cos + partner * jnp.where(first, -sin, sin)


def _diff_prep_body(x_ref, g_ref, cos_ref, sin_ref, o_ref):
    for c in range(D_INNER // LANES):
        x = x_ref[0, :, c * LANES:(c + 1) * LANES].astype(F32)
        o_ref[0, 0, :, c * LANES:(c + 1) * LANES] = _norm_rope(x, g_ref[0], cos_ref[0], sin_ref[0]).astype(o_ref.dtype)


def diff_prep(proj, gains, cos_t, sin_t):
    bsz, seq, _ = proj.shape
    tm = min(OUT_TM, seq)
    return pl.pallas_call(
        _diff_prep_body,
        grid=(2, bsz, seq // tm),
        in_specs=[
            pl.BlockSpec((1, tm, D_INNER), lambda w, b, s: (b, s, w)),
            pl.BlockSpec((1, 1, LANES), lambda w, b, s: (w, 0, 0)),
            pl.BlockSpec((1, tm, LANES), lambda w, b, s: (b, s, 0)),
            pl.BlockSpec((1, tm, LANES), lambda w, b, s: (b, s, 0)),
        ],
        out_specs=pl.BlockSpec((1, 1, tm, D_INNER), lambda w, b, s: (w, b, s, 0)),
        out_shape=jax.ShapeDtypeStruct((2, bsz, seq, D_INNER), BF16),
        compiler_params=_cparams(("parallel", "parallel", "parallel")),
        name="diff_prep",
    )(proj, gains, cos_t, sin_t)


def _diff_epilogue(on, lam, sg, z, scale_out):
    r = on.shape[0] // 2
    o = on[:r] - lam * on[r:]
    o = o * lax.rsqrt(jnp.mean(o * o, axis=-1, keepdims=True) + EPS) * sg * scale_out
    return o * _silu(z.astype(F32))


def _diff_attn_online_body(q_ref, k_ref, v_ref, z_ref, lam_ref, sg_ref, o_ref, *, tq, scale_out):
    qi = pl.program_id(2)
    q = q_ref[0, 0]
    lane = lax.broadcasted_iota(jnp.int32, (tq, LANES), 1)
    zero = jnp.zeros_like(q)
    qq = jnp.concatenate([jnp.where(lane < DIFF_HALF, q, zero), jnp.where(lane >= DIFF_HALF, q, zero)], axis=0)

    def block(j, carry, masked):
        m, l, acc = carry
        k0 = pl.multiple_of(j * tq, tq)
        kb = k_ref[0, 0, pl.ds(k0, tq), :]
        vb = v_ref[0, pl.ds(k0, tq), :]
        s = lax.dot_general(qq, kb, (((1,), (1,)), ((), ())), preferred_element_type=F32)
        if masked:
            r = lax.broadcasted_iota(jnp.int32, (2 * tq, tq), 0)
            r = jnp.where(r >= tq, r - tq, r)
            cidx = lax.broadcasted_iota(jnp.int32, (2 * tq, tq), 1)
            s = jnp.where(cidx <= r, s, -jnp.inf)
        m_new = jnp.maximum(m, jnp.max(s, axis=-1, keepdims=True))
        alpha = jnp.exp2(m - m_new)
        p = jnp.exp2(s - m_new)
        l = alpha * l + jnp.sum(p, axis=-1, keepdims=True)
        acc = alpha * acc + jnp.dot(p.astype(BF16), vb, preferred_element_type=F32)
        return m_new, l, acc

    init = (jnp.full((2 * tq, 1), -jnp.inf, F32), jnp.zeros((2 * tq, 1), F32), jnp.zeros((2 * tq, LANES), F32))
    carry = lax.fori_loop(0, qi, lambda j, cr: block(j, cr, False), init)
    m, l, acc = block(qi, carry, True)
    o_ref[0] = _diff_epilogue(acc / l, lam_ref[...], sg_ref[...], z_ref[0], scale_out).astype(o_ref.dtype)


def _diff_attn_direct_body(qraw_ref, kraw_ref, v_ref, z_ref, g_ref, cos_ref, sin_ref, lam_ref, sg_ref, o_ref,
                           q_ref, k_ref, *, seq, scale_out):
    tq, hq = DIFF_TQ, DIFF_TQ // 2
    for i in range(seq // tq):
        rows = pl.ds(i * tq, tq)
        cos, sin = cos_ref[0, rows, :], sin_ref[0, rows, :]
        q_ref[0, 0, rows, :] = _norm_rope(qraw_ref[0, rows, :].astype(F32), g_ref[0], cos, sin).astype(BF16)
        k_ref[0, 0, rows, :] = _norm_rope(kraw_ref[0, rows, :].astype(F32), g_ref[1], cos, sin).astype(BF16)
    lane = lax.broadcasted_iota(jnp.int32, (tq, LANES), 1)
    tri = (lax.broadcasted_iota(jnp.int32, (2 * hq, hq), 1)
           <= lax.broadcasted_iota(jnp.int32, (2 * hq, hq), 0) % hq)
    nt = (((1,), (1,)), ((), ()))

    def lanesum(p):
        acc = p[:, :LANES]
        for c in range(1, p.shape[1] // LANES):
            acc = acc + p[:, c * LANES:(c + 1) * LANES]
        return acc

    def qtile(i):
        r0 = i * tq
        q = q_ref[0, 0, pl.ds(r0, tq), :]
        zero = jnp.zeros_like(q)
        q0 = jnp.where(lane < DIFF_HALF, q, zero)
        q1 = jnp.where(lane >= DIFF_HALF, q, zero)
        qq = jnp.concatenate([q0[:hq], q1[:hq], q0[hq:], q1[hq:]], axis=0)

        def offdiag(j, carry):
            acc, l = carry
            c0 = j * tq
            p = jnp.exp2(lax.dot_general(qq, k_ref[0, 0, pl.ds(c0, tq), :], nt, preferred_element_type=F32))
            return (acc + jnp.dot(p.astype(BF16), v_ref[0, pl.ds(c0, tq), :], preferred_element_type=F32),
                    l + lanesum(p))

        carry = (jnp.zeros((2 * tq, LANES), F32), jnp.zeros((2 * tq, LANES), F32))
        for j in range(i):
            carry = offdiag(j, carry)
        acc, l = carry
        r1 = r0 + hq
        k_a, v_a = k_ref[0, 0, pl.ds(r0, hq), :], v_ref[0, pl.ds(r0, hq), :]
        k_b, v_b = k_ref[0, 0, pl.ds(r1, hq), :], v_ref[0, pl.ds(r1, hq), :]
        qq_a, qq_b = qq[:tq], qq[tq:]
        p_aa = jnp.where(tri, jnp.exp2(lax.dot_general(qq_a, k_a, nt, preferred_element_type=F32)), 0.0)
        p_ba = jnp.exp2(lax.dot_general(qq_b, k_a, nt, preferred_element_type=F32))
        p_bb = jnp.where(tri, jnp.exp2(lax.dot_general(qq_b, k_b, nt, preferred_element_type=F32)), 0.0)
        acc_a = acc[:tq] + jnp.dot(p_aa.astype(BF16), v_a, preferred_element_type=F32)
        acc_b = (acc[tq:] + jnp.dot(p_ba.astype(BF16), v_a, preferred_element_type=F32)
                 + jnp.dot(p_bb.astype(BF16), v_b, preferred_element_type=F32))
        l_a = jnp.sum(l[:tq] + lanesum(p_aa), axis=-1, keepdims=True)
        l_b = jnp.sum(l[tq:] + lanesum(p_ba) + lanesum(p_bb), axis=-1, keepdims=True)
        o_ref[0, pl.ds(r0, hq), :] = _diff_epilogue(
            acc_a / l_a, lam_ref[...], sg_ref[...], z_ref[0, pl.ds(r0, hq), :], scale_out).astype(o_ref.dtype)
        o_ref[0, pl.ds(r1, hq), :] = _diff_epilogue(
            acc_b / l_b, lam_ref[...], sg_ref[...], z_ref[0, pl.ds(r1, hq), :], scale_out).astype(o_ref.dtype)

    for i in range(seq // tq):
        qtile(i)


def diff_attn(proj, gains, cos_t, sin_t, lam_vec, subln_g, lam_init, score_bound_log2):
    bsz, seq, _ = proj.shape
    koff = D_INNER // LANES
    voff = 2 * D_INNER // LANES
    zoff = 3 * D_INNER // LANES
    sg = subln_g.reshape(1, LANES).astype(F32)
    vec = pl.BlockSpec((1, LANES), lambda *_: (0, 0))

    def online(proj, gains, cos_t, sin_t, lam_vec, sg):
        qk = diff_prep(proj, gains, cos_t, sin_t)
        tq = min(DIFF_TQ, seq)
        return pl.pallas_call(
            functools.partial(_diff_attn_online_body, tq=tq, scale_out=1.0 - lam_init),
            grid=(bsz, DIFF_HEADS, seq // tq),
            in_specs=[
                pl.BlockSpec((1, 1, tq, LANES), lambda b, h, i: (0, b, i, h)),
                pl.BlockSpec((1, 1, seq, LANES), lambda b, h, i: (1, b, 0, h)),
                pl.BlockSpec((1, seq, LANES), lambda b, h, i: (b, 0, voff + h)),
                pl.BlockSpec((1, tq, LANES), lambda b, h, i: (b, i, zoff + h)),
                vec, vec,
            ],
            out_specs=pl.BlockSpec((1, tq, LANES), lambda b, h, i: (b, i, h)),
            out_shape=jax.ShapeDtypeStruct((bsz, seq, D_INNER), BF16),
            compiler_params=_cparams(("parallel", "parallel", "arbitrary")),
            name="diff_attn_online",
        )(qk, qk, proj, proj, lam_vec, sg)

    def direct(proj, gains, cos_t, sin_t, lam_vec, sg):
        return pl.pallas_call(
            functools.partial(_diff_attn_direct_body, seq=seq, scale_out=1.0 - lam_init),
            grid=(bsz, DIFF_HEADS),
            in_specs=[
                pl.BlockSpec((1, seq, LANES), lambda b, h: (b, 0, h)),
                pl.BlockSpec((1, seq, LANES), lambda b, h: (b, 0, koff + h)),
                pl.BlockSpec((1, seq, LANES), lambda b, h: (b, 0, voff + h)),
                pl.BlockSpec((1, seq, LANES), lambda b, h: (b, 0, zoff + h)),
                pl.BlockSpec((2, 1, LANES), lambda b, h: (0, 0, 0)),
                pl.BlockSpec((1, seq, LANES), lambda b, h: (b, 0, 0)),
                pl.BlockSpec((1, seq, LANES), lambda b, h: (b, 0, 0)),
                vec, vec,
            ],
            out_specs=pl.BlockSpec((1, seq, LANES), lambda b, h: (b, 0, h)),
            out_shape=jax.ShapeDtypeStruct((bsz, seq, D_INNER), BF16),
            scratch_shapes=[pltpu.VMEM((1, 1, seq, LANES), BF16), pltpu.VMEM((1, 1, seq, LANES), BF16)],
            compiler_params=_cparams(("parallel", "parallel")),
            name="diff_attn_direct",
        )(proj, proj, proj, proj, gains, cos_t, sin_t, lam_vec, sg)

    operands = (proj, gains, cos_t, sin_t, lam_vec, sg)
    if seq % DIFF_TQ != 0:
        return online(*operands)
    return lax.cond(score_bound_log2 <= DIFF_DIRECT_MAX_LOG2, direct, online, *operands)


def _ssd_body(xbc_ref, z_ref, dt_ref, cw_ref, cb_ref, dtb_ref, alog_ref, dsk_ref, ng_ref, e64_ref, e512_ref,
              o_ref, xs_ref, st_ref, *, ts):
    L = SSD_CHUNK
    gw = SSD_HPG * SSD_HEAD_DIM
    pad = 8

    @pl.when(pl.program_id(1) == 0)
    def _():
        st_ref[...] = jnp.zeros_like(st_ref)
        xs_ref[0:pad, :] = jnp.zeros((pad, SSD_CONV_CH), F32)

    @pl.when(pl.program_id(1) > 0)
    def _():
        xs_ref[0:pad, :] = xs_ref[ts:ts + pad, :]

    xs_ref[pad:pad + ts, :] = xbc_ref[0].astype(F32)
    cw = cw_ref[...]
    acc = cb_ref[...] + xs_ref[pad:pad + ts, :] * cw[SSD_CONV - 1:SSD_CONV, :]
    for w in range(SSD_CONV - 1):
        sh = SSD_CONV - 1 - w
        acc = acc + xs_ref[pad - sh:pad - sh + ts, :] * cw[w:w + 1, :]
    xc_all = _silu(acc)

    tri3 = _tri3(L)
    lane_grp = lax.broadcasted_iota(jnp.int32, (L, LANES), 1) // SSD_HEADS
    row4 = lax.broadcasted_iota(jnp.int32, (L, SSD_HPG * L), 0)
    col4 = lax.broadcasted_iota(jnp.int32, (L, SSD_HPG * L), 1) % L
    eye4 = (row4 == col4).astype(F32)
    causal4 = col4 <= row4
    br = lax.broadcasted_iota(jnp.int32, (SSD_HPG * L, gw), 0) // L
    bc = lax.broadcasted_iota(jnp.int32, (SSD_HPG * L, gw), 1) // SSD_HEAD_DIM
    blockdiag = br == bc
    a_neg = -jnp.exp(alog_ref[...])

    for c in range(ts // L):
        rs = slice(c * L, (c + 1) * L)
        xc = xc_all[rs, :]
        xr = dt_ref[0, rs, :] + dtb_ref[...]
        dt = jnp.maximum(xr, 0.0) + jnp.log(1.0 + jnp.exp(-jnp.abs(xr)))
        cum = _cumsum_rows(tri3, dt * a_neg)
        cum_p = _pack_split3(cum, lane_grp)
        cum64 = jnp.dot(cum_p, e64_ref[...], preferred_element_type=F32)
        dt64 = jnp.dot(_pack_split3(dt, lane_grp), e64_ref[...], preferred_element_type=F32)
        cum512 = jnp.dot(cum_p, e512_ref[...], preferred_element_type=F32)
        last64 = cum64[L - 1:L, :]
        ys = []
        for g in range(SSD_GROUPS):
            xg = xc[:, g * gw:(g + 1) * gw]
            bg = xc[:, D_INNER + g * SSD_STATE:D_INNER + (g + 1) * SSD_STATE]
            cg = xc[:, D_INNER + SSD_GROUPS * SSD_STATE + g * SSD_STATE:
                    D_INNER + SSD_GROUPS * SSD_STATE + (g + 1) * SSD_STATE]
            cgb = cg.astype(BF16)
            cb = lax.dot_general(cgb, bg.astype(BF16), (((1,), (1,)), ((), ())), preferred_element_type=F32)
            ce = cum512[:, g * SSD_HPG * L:(g + 1) * SSD_HPG * L]
            crow = jnp.sum(ce * eye4, axis=0, keepdims=True)
            m = jnp.where(causal4, jnp.exp(jnp.minimum(ce - crow, 0.0)), 0.0) * jnp.concatenate([cb] * SSD_HPG, axis=1)
            cg64 = cum64[:, g * gw:(g + 1) * gw]
            dg64 = dt64[:, g * gw:(g + 1) * gw]
            xdt = xg * dg64
            rhs = jnp.where(blockdiag, jnp.concatenate([xdt] * SSD_HPG, axis=0), 0.0)
            y = jnp.dot(m.astype(BF16), rhs.astype(BF16), preferred_element_type=F32)
            state = st_ref[g]
            y = y + jnp.dot(cgb, state.astype(BF16), preferred_element_type=F32) * jnp.exp(cg64)
            lg = last64[:, g * gw:(g + 1) * gw]
            xw = xg * (jnp.exp(lg - cg64) * dg64)
            st_ref[g] = state * jnp.exp(lg) + jnp.dot(bg.T.astype(BF16), xw.astype(BF16), preferred_element_type=F32)
            ys.append(y + dsk_ref[:, g * gw:(g + 1) * gw] * xg)
        y = jnp.concatenate(ys, axis=1) * _silu(z_ref[0, rs, :].astype(F32))
        y = y * lax.rsqrt(jnp.mean(y * y, axis=-1, keepdims=True) + EPS) * ng_ref[...]
        o_ref[0, rs, :] = y.astype(o_ref.dtype)


def _ssd_expanders():
    e64 = np.zeros((LANES, D_INNER), np.float32)
    e512 = np.zeros((LANES, SSD_HEADS * SSD_CHUNK), np.float32)
    for rep in range(SSD_REPS):
        for h in range(SSD_HEADS):
            e64[rep * SSD_HEADS + h, h * SSD_HEAD_DIM:(h + 1) * SSD_HEAD_DIM] = 1.0
            e512[rep * SSD_HEADS + h, h * SSD_CHUNK:(h + 1) * SSD_CHUNK] = 1.0
    return jnp.asarray(e64, BF16), jnp.asarray(e512, BF16)


def ssd_mixer(proj, dt_aux, conv_w, conv_b, dt_bias, a_log, d_skip, norm_g):
    bsz, seq, _ = proj.shape
    ts = min(SSD_TS, seq)
    e64, e512 = _ssd_expanders()

    def pad128(v):
        return jnp.zeros((1, LANES), F32).at[0, :SSD_REPS * SSD_HEADS].set(jnp.tile(v.astype(F32), SSD_REPS))

    dsk = jnp.repeat(d_skip.astype(F32), SSD_HEAD_DIM).reshape(1, D_INNER)
    full = lambda shape: pl.BlockSpec(shape, lambda b, s: (0,) * len(shape))
    return pl.pallas_call(
        functools.partial(_ssd_body, ts=ts),
        grid=(bsz, seq // ts),
        in_specs=[
            pl.BlockSpec((1, ts, SSD_CONV_CH), lambda b, s: (b, s, 0)),
            pl.BlockSpec((1, ts, D_INNER), lambda b, s: (b, s, 2)),
            pl.BlockSpec((1, ts, LANES), lambda b, s: (b, s, 0)),
            full((SSD_CONV, SSD_CONV_CH)),
            full((1, SSD_CONV_CH)),
            full((1, LANES)),
            full((1, LANES)),
            full((1, D_INNER)),
            full((1, D_INNER)),
            full((LANES, D_INNER)),
            full((LANES, SSD_HEADS * SSD_CHUNK)),
        ],
        out_specs=pl.BlockSpec((1, ts, D_INNER), lambda b, s: (b, s, 0)),
        out_shape=jax.ShapeDtypeStruct((bsz, seq, D_INNER), BF16),
        scratch_shapes=[pltpu.VMEM((ts + 16, SSD_CONV_CH), F32),
                        pltpu.VMEM((SSD_GROUPS, SSD_STATE, SSD_HPG * SSD_HEAD_DIM), F32)],
        compiler_params=_cparams(("parallel", "arbitrary")),
        name="ssd_mixer",
    )(proj, proj, dt_aux, conv_w.astype(F32), conv_b.reshape(1, -1).astype(F32), pad128(dt_bias), pad128(a_log),
      dsk, norm_g.reshape(1, -1).astype(F32), e64, e512)


def _pad_cols(w, width=LANES):
    return jnp.zeros((w.shape[0], width), F32).at[:, :w.shape[1]].set(w.astype(F32))


def s5_layer(x, norm_g, w_in, lam_re, lam_im, log_step, b_re, b_im, c_re, c_im, d_skip, w_glu):
    bsz, seq, d = x.shape
    proj = in_proj(x.reshape(bsz * seq, d), norm_g, w_in.astype(BF16)).reshape(bsz, seq, -1)
    y = s5_slab(proj, *s5_slab_params(lam_re, lam_im, log_step, b_re, b_im, c_re, c_im))
    branch = s5_post(y, proj, d_skip, w_glu.astype(BF16))
    return branch, proj, (2 * D_INNER) // D_X, (2 * D_INNER) // D_X + 1


def gla_layer(x, norm_g, w_in, w_gk2, b_gk2, gla_norm_g):
    bsz, seq, d = x.shape
    nqk = GLA_HEADS * GLA_DK
    c0 = 2 * nqk + D_INNER
    w_main = jnp.concatenate([w_in[:, :c0], w_in[:, c0 + GLA_RANK:]], axis=1).astype(BF16)
    proj, gk_aux = in_proj(x.reshape(bsz * seq, d), norm_g, w_main, _pad_cols(w_in[:, c0:c0 + GLA_RANK]))
    proj = proj.reshape(bsz, seq, -1)
    branch = gla_mixer(proj, gk_aux.reshape(bsz, seq, LANES), w_gk2, b_gk2, gla_norm_g)
    zx_col = (c0 + D_INNER) // D_X
    return branch, proj, zx_col, zx_col + 1


def diff_layer(x, norm_g, w_in, cos_t, sin_t, q_g, k_g, lq1, lk1, lq2, lk2, subln_g, lam_init):
    bsz, seq, d = x.shape
    proj = in_proj(x.reshape(bsz * seq, d), norm_g, w_in.astype(BF16)).reshape(bsz, seq, -1)
    q_scale = (DIFF_HALF ** -0.5) * math.log2(math.e)
    gains = jnp.stack([jnp.tile(q_g.astype(F32), LANES // DIFF_HALF) * q_scale,
                       jnp.tile(k_g.astype(F32), LANES // DIFF_HALF)]).reshape(2, 1, LANES)
    score_bound_log2 = (BF16_ROUNDING_SLACK * DIFF_HALF * q_scale
                        * jnp.max(jnp.abs(q_g.astype(F32))) * jnp.max(jnp.abs(k_g.astype(F32))))
    lam = (jnp.exp(jnp.sum(lq1.astype(F32) * lk1.astype(F32)))
           - jnp.exp(jnp.sum(lq2.astype(F32) * lk2.astype(F32))) + lam_init)
    lam_vec = jnp.full((1, LANES), lam, F32)
    branch = diff_attn(proj, gains, cos_t, sin_t, lam_vec, subln_g, lam_init, score_bound_log2)
    zx_col = (4 * D_INNER) // D_X
    return branch, proj, zx_col, zx_col + 1


def ssd_layer(x, norm_g, w_in, conv_w, conv_b, dt_bias, a_log, d_skip, ssd_norm_g):
    bsz, seq, d = x.shape
    c0 = SSD_CONV_CH
    w_main = jnp.concatenate([w_in[:, :c0], w_in[:, c0 + SSD_HEADS:]], axis=1).astype(BF16)
    w_dt = jnp.tile(w_in[:, c0:c0 + SSD_HEADS], (1, SSD_REPS))
    proj, dt_aux = in_proj(x.reshape(bsz * seq, d), norm_g, w_main, _pad_cols(w_dt))
    proj = proj.reshape(bsz, seq, -1)
    branch = ssd_mixer(proj, dt_aux.reshape(bsz, seq, LANES), conv_w, conv_b, dt_bias, a_log, d_skip, ssd_norm_g)
    zx_col = (c0 + D_INNER) // D_X
    return branch, proj, zx_col, zx_col + 1


def rope_tables(positions):
    inv = ROPE_THETA ** (-jnp.arange(0, DIFF_HALF, 2, dtype=F32) / DIFF_HALF)
    ang = positions.astype(F32)[..., None] * inv
    reps = LANES // (DIFF_HALF // 2)
    return jnp.tile(jnp.cos(ang), (1, 1, reps)), jnp.tile(jnp.sin(ang), (1, 1, reps))


def kernel(x, mem, positions, norm_g, w_out, mem_norm_g, w_mem_kv, xq_g, xk_g, s5_w_in, s5_lam_re, s5_lam_im, s5_log_step, s5_b_re, s5_b_im, s5_c_re, s5_c_im, s5_d, s5_w_glu, gla_w_in, gla_w_gk2, gla_b_gk2, gla_norm_g, diff_w_in, diff_q_g, diff_k_g, diff_lq1, diff_lk1, diff_lq2, diff_lk2, diff_subln_g, ssd_w_in, ssd_conv_w, ssd_conv_b, ssd_dt_bias, ssd_a_log, ssd_d, ssd_norm_g):
    depth = norm_g.shape[0]
    k_all, v_all = mem_kv(mem, mem_norm_g, w_mem_kv, xk_g)
    cos_t, sin_t = rope_tables(positions)
    w_out_b = w_out.astype(BF16)
    for i in range(depth):
        kind, j = i % 4, i // 4
        if kind == 0:
            branch, proj, zc, qc = s5_layer(x, norm_g[i], s5_w_in[j], s5_lam_re[j], s5_lam_im[j], s5_log_step[j],
                                            s5_b_re[j], s5_b_im[j], s5_c_re[j], s5_c_im[j], s5_d[j], s5_w_glu[j])
        elif kind == 1:
            branch, proj, zc, qc = gla_layer(x, norm_g[i], gla_w_in[j], gla_w_gk2[j], gla_b_gk2[j], gla_norm_g[j])
        elif kind == 2:
            lam_init = 0.8 - 0.6 * math.exp(-0.3 * i)
            branch, proj, zc, qc = diff_layer(x, norm_g[i], diff_w_in[j], cos_t, sin_t, diff_q_g[j], diff_k_g[j],
                                              diff_lq1[j], diff_lk1[j], diff_lq2[j], diff_lk2[j], diff_subln_g[j],
                                              lam_init)
        else:
            branch, proj, zc, qc = ssd_layer(x, norm_g[i], ssd_w_in[j], ssd_conv_w[j], ssd_conv_b[j], ssd_dt_bias[j],
                                             ssd_a_log[j], ssd_d[j], ssd_norm_g[j])
        x = out_stage(branch, proj, zc, qc, k_all[i], v_all[i], x, w_out_b[i], xq_g[i])
    return x
```

```python
import functools
import math

import numpy as np
import jax
import jax.numpy as jnp
from jax import lax
from jax.experimental import pallas as pl
from jax.experimental.pallas import tpu as pltpu

F32 = jnp.float32
BF16 = jnp.bfloat16

D_MODEL = 1024
D_INNER = 2048
EPS = 1e-6
MEM_LEN = 256
X_HEADS = 4
X_HEAD_DIM = 128
D_X = X_HEADS * X_HEAD_DIM
D_GATE = D_INNER + D_X
S5_GROUP = 16
S5_GROUPS = D_INNER // S5_GROUP
S5_STATE = 64
GLA_HEADS = 4
GLA_DK = 128
GLA_DV = 512
GLA_RANK = 16
GLA_TAU = 16.0
DIFF_HEADS = 16
DIFF_HALF = 64
DIFF_VDIM = 128
ROPE_THETA = 10000.0
SSD_HEAD_DIM = 64
SSD_HEADS = 32
SSD_GROUPS = 8
SSD_HPG = 4
SSD_STATE = 128
SSD_CONV = 4
SSD_CONV_CH = D_INNER + 2 * SSD_GROUPS * SSD_STATE

LANES = 128
MXU_WIDTH = 256
VMEM_LIMIT = 48 * 1024 * 1024

PROJ_TM = 1024
PROJ_TN_MAX = 2560
OUT_TM = 1024
POST_TM = 512
ROW_SUB = 128
S5_BT = 16
S5_T = 16
S5_TS = 512
GLA_TS = 512
GLA_CHUNK = 128
GLA_SUB = 16
GLA_SAFE_DECAY = 40.0
DIFF_TQ = 512
DIFF_DIRECT_MAX_LOG2 = 64.0
BF16_ROUNDING_SLACK = 1.02
SSD_TS = 256
SSD_CHUNK = 128
SSD_REPS = 3


def _cparams(sem):
    return pltpu.CompilerParams(dimension_semantics=sem, vmem_limit_bytes=VMEM_LIMIT)


def _sigmoid(x):
    return 1.0 / (1.0 + jnp.exp(-x))


def _silu(x):
    return x * _sigmoid(x)


def _split3(x):
    x1 = x.astype(BF16)
    r1 = x - x1.astype(F32)
    x2 = r1.astype(BF16)
    x3 = (r1 - x2.astype(F32)).astype(BF16)
    return x1, x2, x3


def _cumsum_rows(tri3, x):
    return jnp.dot(tri3, jnp.concatenate(_split3(x), axis=0), preferred_element_type=F32)


def _pack_split3(x, lane_grp):
    x1, x2, x3 = _split3(x)
    zero = jnp.zeros_like(x1)
    return jnp.where(lane_grp == 0, x1, jnp.where(lane_grp == 1, x2, jnp.where(lane_grp == 2, x3, zero)))


def _tri3(n):
    row = lax.broadcasted_iota(jnp.int32, (n, n), 0)
    col = lax.broadcasted_iota(jnp.int32, (n, n), 1)
    tri = (col <= row).astype(BF16)
    return jnp.concatenate([tri, tri, tri], axis=1)


def _in_proj_body(x_ref, g_ref, w_ref, *rest, has_aux):
    if has_aux:
        waux_ref, o_ref, oaux_ref, h_ref = rest
    else:
        o_ref, h_ref = rest

    @pl.when(pl.program_id(1) == 0)
    def _():
        x = x_ref[...]
        h = x * lax.rsqrt(jnp.mean(x * x, axis=-1, keepdims=True) + EPS) * g_ref[...]
        hb = h.astype(BF16)
        h_ref[...] = hb
        if has_aux:
            oaux_ref[...] = jnp.dot(hb, waux_ref[...], preferred_element_type=F32)

    o_ref[...] = jnp.dot(h_ref[...], w_ref[...], preferred_element_type=F32).astype(o_ref.dtype)


def in_proj(x2d, g, w_bf16, w_aux=None):
    n, d = x2d.shape
    nc = w_bf16.shape[1]
    tm = min(PROJ_TM, n)
    tn = max(t for t in range(MXU_WIDTH, PROJ_TN_MAX + 1, MXU_WIDTH) if nc % t == 0)
    assert n % tm == 0
    has_aux = w_aux is not None
    in_specs = [
        pl.BlockSpec((tm, d), lambda i, j: (i, 0)),
        pl.BlockSpec((1, d), lambda i, j: (0, 0)),
        pl.BlockSpec((d, tn), lambda i, j: (0, j)),
    ]
    args = [x2d, g.reshape(1, d).astype(F32), w_bf16]
    out_shape = [jax.ShapeDtypeStruct((n, nc), BF16)]
    out_specs = [pl.BlockSpec((tm, tn), lambda i, j: (i, j))]
    if has_aux:
        in_specs.append(pl.BlockSpec((d, LANES), lambda i, j: (0, 0)))
        args.append(w_aux.astype(BF16))
        out_shape.append(jax.ShapeDtypeStruct((n, LANES), F32))
        out_specs.append(pl.BlockSpec((tm, LANES), lambda i, j: (i, 0)))
    res = pl.pallas_call(
        functools.partial(_in_proj_body, has_aux=has_aux),
        grid=(n // tm, nc // tn),
        in_specs=in_specs,
        out_specs=out_specs,
        out_shape=out_shape,
        scratch_shapes=[pltpu.VMEM((tm, d), BF16)],
        compiler_params=_cparams(("parallel", "arbitrary")),
        name="in_proj_aux" if has_aux else "in_proj",
    )(*args)
    return (res[0], res[1]) if has_aux else res[0]


def _mem_kv_body(mem_ref, mg_ref, w_ref, kg_ref, k_ref, v_ref):
    m = mem_ref[0]
    mn = m * lax.rsqrt(jnp.mean(m * m, axis=-1, keepdims=True) + EPS) * mg_ref[...]
    kv = jnp.dot(mn.astype(BF16), w_ref[0], preferred_element_type=F32)
    kg = kg_ref[0]
    for h in range(X_HEADS):
        kh = kv[:, h * X_HEAD_DIM:(h + 1) * X_HEAD_DIM]
        kh = kh * lax.rsqrt(jnp.mean(kh * kh, axis=-1, keepdims=True) + EPS) * kg
        k_ref[0, 0, :, h * X_HEAD_DIM:(h + 1) * X_HEAD_DIM] = kh.astype(BF16)
    v_ref[0, 0] = kv[:, D_X:].astype(BF16)


def mem_kv(mem, mem_norm_g, w_mem_kv, xk_g):
    bsz, mlen, d = mem.shape
    depth = w_mem_kv.shape[0]
    return pl.pallas_call(
        _mem_kv_body,
        grid=(depth, bsz),
        in_specs=[
            pl.BlockSpec((1, mlen, d), lambda l, b: (b, 0, 0)),
            pl.BlockSpec((1, d), lambda l, b: (0, 0)),
            pl.BlockSpec((1, d, 2 * D_X), lambda l, b: (l, 0, 0)),
            pl.BlockSpec((1, 1, X_HEAD_DIM), lambda l, b: (l, 0, 0)),
        ],
        out_specs=[
            pl.BlockSpec((1, 1, mlen, D_X), lambda l, b: (l, b, 0, 0)),
            pl.BlockSpec((1, 1, mlen, D_X), lambda l, b: (l, b, 0, 0)),
        ],
        out_shape=[jax.ShapeDtypeStruct((depth, bsz, mlen, D_X), BF16)] * 2,
        compiler_params=_cparams(("parallel", "parallel")),
        name="mem_kv",
    )(mem, mem_norm_g.reshape(1, d).astype(F32), w_mem_kv.astype(BF16),
      xk_g.reshape(depth, 1, X_HEAD_DIM).astype(F32))


def _out_body(br_ref, zx_ref, qx_ref, k_ref, v_ref, x_ref, w_ref, qg_ref, o_ref):
    q = qx_ref[0].astype(F32)
    zx = zx_ref[0].astype(F32)
    qg = qg_ref[...]
    outs = []
    for h in range(X_HEADS):
        sl = slice(h * X_HEAD_DIM, (h + 1) * X_HEAD_DIM)
        qh = q[:, sl]
        qn = qh * lax.rsqrt(jnp.mean(qh * qh, axis=-1, keepdims=True) + EPS) * qg
        s = lax.dot_general(qn.astype(BF16), k_ref[0, :, sl], (((1,), (1,)), ((), ())),
                            preferred_element_type=F32) * (X_HEAD_DIM ** -0.5)
        p = jnp.exp(s - jnp.max(s, axis=-1, keepdims=True))
        l = jnp.sum(p, axis=-1, keepdims=True)
        oh = jnp.dot(p.astype(BF16), v_ref[0, :, sl], preferred_element_type=F32) / l
        outs.append((oh * _silu(zx[:, sl])).astype(BF16))
    mem_out = jnp.concatenate(outs, axis=-1)
    acc = jnp.dot(br_ref[0], w_ref[:D_INNER, :], preferred_element_type=F32)
    acc = acc + jnp.dot(mem_out, w_ref[D_INNER:, :], preferred_element_type=F32)
    o_ref[0] = x_ref[0] + acc


def out_stage(branch, proj, zx_col, qx_col, k_l, v_l, x, w_out_bf16, xq_g):
    bsz, seq, _ = branch.shape
    tm = min(OUT_TM, seq)
    return pl.pallas_call(
        _out_body,
        grid=(bsz, seq // tm),
        in_specs=[
            pl.BlockSpec((1, tm, D_INNER), lambda b, s: (b, s, 0)),
            pl.BlockSpec((1, tm, D_X), lambda b, s: (b, s, zx_col)),
            pl.BlockSpec((1, tm, D_X), lambda b, s: (b, s, qx_col)),
            pl.BlockSpec((1, MEM_LEN, D_X), lambda b, s: (b, 0, 0)),
            pl.BlockSpec((1, MEM_LEN, D_X), lambda b, s: (b, 0, 0)),
            pl.BlockSpec((1, tm, D_MODEL), lambda b, s: (b, s, 0)),
            pl.BlockSpec((D_GATE, D_MODEL), lambda b, s: (0, 0)),
            pl.BlockSpec((1, X_HEAD_DIM), lambda b, s: (0, 0)),
        ],
        out_specs=pl.BlockSpec((1, tm, D_MODEL), lambda b, s: (b, s, 0)),
        out_shape=jax.ShapeDtypeStruct((bsz, seq, D_MODEL), F32),
        compiler_params=_cparams(("parallel", "parallel")),
        name="out_stage",
    )(branch, proj, proj, k_l, v_l, x, w_out_bf16, xq_g.reshape(1, X_HEAD_DIM).astype(F32))


def _s5_slab_body(u_ref, bd_ref, mc_ref, nc_ref, a_ref, y_ref, w_ref, m_ref, n_ref, xs_ref, sp_ref, st_ref, *, nc, bt):
    T = S5_T
    gps = LANES // S5_GROUP
    half = S5_STATE * gps

    @pl.when((pl.program_id(1) == 0) & (pl.program_id(2) == 0))
    def _():
        for s in range(T):
            for t in range(max(s - 1, 0), T):
                blk = bd_ref[0, t - s] if t >= s else jnp.zeros((LANES, LANES), BF16)
                w_ref[s * LANES:(s + 1) * LANES, t * LANES:(t + 1) * LANES] = blk
        mrow = lax.broadcasted_iota(jnp.int32, (T * LANES, half), 0) // S5_GROUP % gps
        mcol = lax.broadcasted_iota(jnp.int32, (T * LANES, half), 1) // S5_STATE
        nrow = lax.broadcasted_iota(jnp.int32, (half, T * LANES), 0) // S5_STATE
        ncol = lax.broadcasted_iota(jnp.int32, (half, T * LANES), 1) // S5_GROUP % gps
        for part in range(2):
            mt = jnp.concatenate([mc_ref[0, part]] * (half // LANES), axis=1)
            m_ref[:, part * half:(part + 1) * half] = jnp.where(mrow == mcol, mt, jnp.zeros_like(mt))
            nt = jnp.concatenate([nc_ref[0, part]] * gps, axis=0)
            n_ref[part * half:(part + 1) * half, :] = jnp.where(nrow == ncol, nt, jnp.zeros_like(nt))

    @pl.when(pl.program_id(2) == 0)
    def _():
        st_ref[...] = jnp.zeros_like(st_ref)

    v4 = pltpu.einshape("bcsl->csbl", u_ref[...].reshape(bt, nc, T, LANES))
    v = jnp.concatenate([v4[:, s].reshape(nc * bt, LANES) for s in range(T)], axis=1)
    xs_ref[...] = jnp.dot(v, m_ref[...], preferred_element_type=F32)
    a = a_ref[0]
    a_re, a_im = a[:, :half], a[:, half:]

    def step(c, carry):
        sr, si = carry
        r0 = pl.multiple_of(c * bt, bt)
        sp_ref[pl.ds(r0, bt), :half] = sr
        sp_ref[pl.ds(r0, bt), half:] = si
        x = xs_ref[pl.ds(r0, bt), :]
        return a_re * sr - a_im * si + x[:, :half], a_re * si + a_im * sr + x[:, half:]

    st = st_ref[...]
    sr, si = lax.fori_loop(0, nc, step, (st[:, :half], st[:, half:]), unroll=4)
    st_ref[:, :half] = sr
    st_ref[:, half:] = si
    y_state = jnp.dot(sp_ref[...].astype(BF16), n_ref[...], preferred_element_type=F32)
    cols = []
    for tp in range(T // 2):
        kk = (2 * tp + 2) * LANES
        cols.append(jnp.dot(v[:, :kk], w_ref[:kk, tp * 2 * LANES:(tp + 1) * 2 * LANES], preferred_element_type=F32))
    y = (y_state + jnp.concatenate(cols, axis=1)).astype(BF16)
    y4 = jnp.stack([y[:, t * LANES:(t + 1) * LANES].reshape(nc, bt, LANES) for t in range(T)], axis=1)
    y_ref[...] = pltpu.einshape("ctbl->bctl", y4).reshape(bt, nc * T, LANES)


def s5_slab(proj, bd, m_compact, n_compact, a_slab):
    bsz, seq, _ = proj.shape
    bt = S5_BT
    ts = min(S5_TS, seq)
    nslab = D_INNER // LANES
    nstate = 2 * S5_STATE * (LANES // S5_GROUP)
    assert bsz % bt == 0 and seq % ts == 0
    return pl.pallas_call(
        functools.partial(_s5_slab_body, nc=ts // S5_T, bt=bt),
        grid=(nslab, bsz // bt, seq // ts),
        in_specs=[
            pl.BlockSpec((bt, ts, LANES), lambda j, b, s: (b, s, j)),
            pl.BlockSpec((1, S5_T, LANES, LANES), lambda j, b, s: (j, 0, 0, 0)),
            pl.BlockSpec((1, 2, S5_T * LANES, LANES), lambda j, b, s: (j, 0, 0, 0)),
            pl.BlockSpec((1, 2, S5_STATE, S5_T * LANES), lambda j, b, s: (j, 0, 0, 0)),
            pl.BlockSpec((1, bt, nstate), lambda j, b, s: (j, 0, 0)),
        ],
        out_specs=pl.BlockSpec((bt, ts, LANES), lambda j, b, s: (b, s, j)),
        out_shape=jax.ShapeDtypeStruct((bsz, seq, D_INNER), BF16),
        scratch_shapes=[pltpu.VMEM((S5_T * LANES, S5_T * LANES), BF16),
                        pltpu.VMEM((S5_T * LANES, nstate), BF16),
                        pltpu.VMEM((nstate, S5_T * LANES), BF16),
                        pltpu.VMEM((ts // S5_T * bt, nstate), F32),
                        pltpu.VMEM((ts // S5_T * bt, nstate), F32),
                        pltpu.VMEM((bt, nstate), F32)],
        compiler_params=_cparams(("arbitrary", "arbitrary", "arbitrary")),
        name="s5_slab",
    )(proj, bd, m_compact, n_compact, a_slab)


def s5_slab_params(lam_re, lam_im, log_step, b_re, b_im, c_re, c_im):
    T, G, P, H = S5_T, S5_GROUPS, S5_STATE, S5_GROUP
    gps = LANES // H
    nslab = G // gps
    hp = lax.Precision.HIGHEST
    step = jnp.exp(log_step.astype(F32))[:, None]
    lam_re = lam_re.astype(F32)
    lam_im = lam_im.astype(F32)
    mag = jnp.exp(lam_re * step)
    lb_re = mag * jnp.cos(lam_im * step)
    lb_im = mag * jnp.sin(lam_im * step)
    den = lam_re * lam_re + lam_im * lam_im
    nr = lb_re - 1.0
    co_re = (nr * lam_re + lb_im * lam_im) / den
    co_im = (lb_im * lam_re - nr * lam_im) / den
    b_re = b_re.astype(F32)
    b_im = b_im.astype(F32)
    bb_re = co_re[..., None] * b_re - co_im[..., None] * b_im
    bb_im = co_re[..., None] * b_im + co_im[..., None] * b_re
    c_re = c_re.astype(F32)
    c_im = c_im.astype(F32)
    j = jnp.arange(T + 1, dtype=F32)[:, None, None]
    pmag = jnp.exp(j * (lam_re * step))
    pw_re = pmag * jnp.cos(j * (lam_im * step))
    pw_im = pmag * jnp.sin(j * (lam_im * step))
    cl_re = c_re[None] * pw_re[:, :, None, :] - c_im[None] * pw_im[:, :, None, :]
    cl_im = c_re[None] * pw_im[:, :, None, :] + c_im[None] * pw_re[:, :, None, :]
    kj = jnp.einsum('gjkp,gph->jgkh', jnp.concatenate([cl_re, -cl_im], axis=-1).transpose(1, 0, 2, 3),
                    jnp.concatenate([bb_re, bb_im], axis=1), precision=hp)
    eye = jnp.eye(gps, dtype=F32)
    bd = jnp.einsum('djgkh,gq->jdghqk', kj[:T].reshape(T, nslab, gps, H, H), eye).reshape(nslab, T, LANES, LANES)
    rev = T - 1 - jnp.arange(T)
    m_re = pw_re[rev][..., None] * bb_re[None] - pw_im[rev][..., None] * bb_im[None]
    m_im = pw_re[rev][..., None] * bb_im[None] + pw_im[rev][..., None] * bb_re[None]

    def m_rows(m):
        r = m.reshape(T, nslab, gps, P, H).transpose(1, 0, 2, 4, 3).reshape(nslab, T * LANES, P)
        return jnp.concatenate([r, r], axis=-1)

    m_compact = jnp.stack([m_rows(m_re), m_rows(m_im)], axis=1)

    def n_rows(cl):
        return cl[1:].reshape(T, nslab, gps, H, P).transpose(1, 4, 0, 2, 3).reshape(nslab, P, T * LANES)

    n_compact = jnp.stack([n_rows(cl_re), -n_rows(cl_im)], axis=1)
    a = jnp.concatenate([pw_re[T].reshape(nslab, gps * P), pw_im[T].reshape(nslab, gps * P)], axis=-1)
    a_slab = jnp.broadcast_to(a[:, None, :], (nslab, S5_BT, 2 * gps * P))
    return bd.astype(BF16), m_compact.astype(BF16), n_compact.astype(BF16), a_slab


def _s5_post_body(y_ref, u_ref, z_ref, d_ref, w_ref, o_ref):
    for r in range(y_ref.shape[1] // ROW_SUB):
        rs = pl.ds(r * ROW_SUB, ROW_SUB)
        yv = y_ref[0, rs, :].astype(F32) + d_ref[...] * u_ref[0, rs, :].astype(F32)
        g = 0.5 * yv * (1.0 + jnp.tanh(math.sqrt(2.0 / math.pi) * (yv + 0.044715 * (yv * yv * yv))))
        gate = _sigmoid(jnp.dot(g.astype(BF16), w_ref[...], preferred_element_type=F32))
        o_ref[0, rs, :] = (g * gate * _silu(z_ref[0, rs, :].astype(F32))).astype(o_ref.dtype)


def s5_post(y, proj, d_skip, w_glu_bf16):
    bsz, seq, _ = y.shape
    tm = min(POST_TM, seq)
    return pl.pallas_call(
        _s5_post_body,
        grid=(bsz, seq // tm),
        in_specs=[
            pl.BlockSpec((1, tm, D_INNER), lambda b, s: (b, s, 0)),
            pl.BlockSpec((1, tm, D_INNER), lambda b, s: (b, s, 0)),
            pl.BlockSpec((1, tm, D_INNER), lambda b, s: (b, s, 1)),
            pl.BlockSpec((1, D_INNER), lambda b, s: (0, 0)),
            pl.BlockSpec((D_INNER, D_INNER), lambda b, s: (0, 0)),
        ],
        out_specs=pl.BlockSpec((1, tm, D_INNER), lambda b, s: (b, s, 0)),
        out_shape=jax.ShapeDtypeStruct((bsz, seq, D_INNER), BF16),
        compiler_params=_cparams(("parallel", "parallel")),
        name="s5_post",
    )(y, proj, proj, d_skip.reshape(1, D_INNER).astype(F32), w_glu_bf16)


def _gla_body(q_ref, k_ref, v_ref, z_ref, gk_ref, w2_ref, b2_ref, ng_ref, o_ref, st_ref, att_ref, *, ts):
    C, R = GLA_CHUNK, GLA_SUB
    nsub = C // R
    nt = (((1,), (1,)), ((), ()))

    @pl.when(pl.program_id(2) == 0)
    def _():
        st_ref[...] = jnp.zeros_like(st_ref)

    tri3 = _tri3(C)
    row_r = lax.broadcasted_iota(jnp.int32, (R, C), 0)
    col_r = lax.broadcasted_iota(jnp.int32, (R, C), 1)

    nchunk = ts // C
    qs_all, ks_all, bs_all, decays = [], [], [], []
    for c in range(nchunk):
        rs = slice(c * C, (c + 1) * C)
        x = jnp.dot(gk_ref[0, rs, :].astype(BF16), w2_ref[...], preferred_element_type=F32) + b2_ref[...]
        g = (jnp.minimum(x, 0.0) - jnp.log(1.0 + jnp.exp(-jnp.abs(x)))) * (1.0 / GLA_TAU)
        bs_all.append(_cumsum_rows(tri3, g))
        qs_all.append(q_ref[0, rs, :].astype(F32) * (GLA_DK ** -0.5))
        ks_all.append(k_ref[0, rs, :].astype(F32))
        decays.append(jnp.max(-jnp.sum(g.reshape(nsub, R, GLA_DK), axis=1)))
    sub_decay = functools.reduce(jnp.maximum, decays)

    def scores_matmul(c, i, cap):
        q, k, b = qs_all[c], ks_all[c], bs_all[c]
        bi = b[i * R:(i + 1) * R, :]
        ref = b[i * R - 1:i * R, :] if i > 0 else jnp.zeros((1, GLA_DK), F32)
        qs = q[i * R:(i + 1) * R, :] * jnp.exp(bi - ref)
        ks = k * jnp.exp(jnp.minimum(ref - b, cap))
        return lax.dot_general(qs.astype(BF16), ks.astype(BF16), nt, preferred_element_type=F32)

    @pl.when(sub_decay <= GLA_SAFE_DECAY)
    def _():
        for c in range(nchunk):
            for i in range(nsub):
                att_ref[c, i * R:(i + 1) * R, :] = jnp.where(col_r <= row_r + i * R,
                                                             scores_matmul(c, i, GLA_SAFE_DECAY), 0.0)

    @pl.when(sub_decay > GLA_SAFE_DECAY)
    def _():
        for c in range(nchunk):
            q, k, b = qs_all[c], ks_all[c], bs_all[c]
            for i in range(nsub):
                bi = b[i * R:(i + 1) * R, :]
                qi = q[i * R:(i + 1) * R, :]
                ki = k[i * R:(i + 1) * R, :]
                dg = jnp.zeros((R, C), F32)
                for s in range(R):
                    w = jnp.exp(jnp.minimum(bi - bi[s:s + 1, :], 0.0))
                    cs = jnp.sum(qi * ki[s:s + 1, :] * w, axis=-1, keepdims=True)
                    dg = jnp.where((col_r == i * R + s) & (row_r >= s), cs, dg)
                if i > 0:
                    dg = dg + jnp.where(col_r < i * R, scores_matmul(c, i, 0.0), 0.0)
                att_ref[c, i * R:(i + 1) * R, :] = dg

    for c in range(nchunk):
        rs = slice(c * C, (c + 1) * C)
        q, k, b = qs_all[c], ks_all[c], bs_all[c]
        v = v_ref[0, rs, :]
        state = st_ref[...]
        o = jnp.dot((q * jnp.exp(b)).astype(BF16), state.astype(BF16), preferred_element_type=F32)
        o = o + jnp.dot(att_ref[c].astype(BF16), v, preferred_element_type=F32)

        b_last = b[C - 1:C, :]
        kd = k * jnp.exp(b_last - b)
        dec = jnp.broadcast_to(jnp.exp(b_last), (C, GLA_DK)).T
        upd = jnp.dot(kd.T.astype(BF16), v, preferred_element_type=F32)
        st_ref[...] = state * jnp.concatenate([dec] * (GLA_DV // GLA_DK), axis=1) + upd

        on = o * lax.rsqrt(jnp.mean(o * o, axis=-1, keepdims=True) + EPS) * ng_ref[...]
        o_ref[0, rs, :] = (on * _silu(z_ref[0, rs, :].astype(F32))).astype(o_ref.dtype)


def gla_mixer(proj, gk_aux, w_gk2, b_gk2, norm_g):
    bsz, seq, _ = proj.shape
    ts = min(GLA_TS, seq)
    w2 = jnp.zeros((LANES, GLA_HEADS * GLA_DK), F32).at[:GLA_RANK].set(w_gk2.astype(F32)).astype(BF16)
    return pl.pallas_call(
        functools.partial(_gla_body, ts=ts),
        grid=(bsz, GLA_HEADS, seq // ts),
        in_specs=[
            pl.BlockSpec((1, ts, GLA_DK), lambda b, h, s: (b, s, h)),
            pl.BlockSpec((1, ts, GLA_DK), lambda b, h, s: (b, s, GLA_HEADS + h)),
            pl.BlockSpec((1, ts, GLA_DV), lambda b, h, s: (b, s, 2 + h)),
            pl.BlockSpec((1, ts, GLA_DV), lambda b, h, s: (b, s, 6 + h)),
            pl.BlockSpec((1, ts, LANES), lambda b, h, s: (b, s, 0)),
            pl.BlockSpec((LANES, GLA_DK), lambda b, h, s: (0, h)),
            pl.BlockSpec((1, GLA_DK), lambda b, h, s: (0, h)),
            pl.BlockSpec((1, GLA_DV), lambda b, h, s: (0, 0)),
        ],
        out_specs=pl.BlockSpec((1, ts, GLA_DV), lambda b, h, s: (b, s, h)),
        out_shape=jax.ShapeDtypeStruct((bsz, seq, D_INNER), BF16),
        scratch_shapes=[pltpu.VMEM((GLA_DK, GLA_DV), F32), pltpu.VMEM((ts // GLA_CHUNK, GLA_CHUNK, GLA_CHUNK), F32)],
        compiler_params=_cparams(("parallel", "parallel", "arbitrary")),
        name="gla_mixer",
    )(proj, proj, proj, proj, gk_aux, w2, b_gk2.reshape(1, -1).astype(F32), norm_g.reshape(1, -1).astype(F32))


def _norm_rope(x, g, cos, sin):
    lane = lax.broadcasted_iota(jnp.int32, (1, LANES), 1)
    first = (lane % DIFF_HALF) < (DIFF_HALF // 2)
    gr = lax.broadcasted_iota(jnp.int32, (LANES, LANES), 0) // DIFF_HALF
    gc = lax.broadcasted_iota(jnp.int32, (LANES, LANES), 1) // DIFF_HALF
    gmat = jnp.where(gr == gc, 1.0 / DIFF_HALF, 0.0).astype(BF16)
    x2 = x * x
    x2h = x2.astype(BF16)
    x2l = (x2 - x2h.astype(F32)).astype(BF16)
    ms = jnp.dot(x2h, gmat, preferred_element_type=F32) + jnp.dot(x2l, gmat, preferred_element_type=F32)
    xn = x * lax.rsqrt(ms + EPS) * g
    partner = jnp.where(first, pltpu.roll(xn, LANES - DIFF_HALF // 2, 1), pltpu.roll(xn, DIFF_HALF // 2, 1))
    return xn * cos + partner * jnp.where(first, -sin, sin)


def _diff_prep_body(x_ref, g_ref, cos_ref, sin_ref, o_ref):
    for c in range(D_INNER // LANES):
        x = x_ref[0, :, c * LANES:(c + 1) * LANES].astype(F32)
        o_ref[0, 0, :, c * LANES:(c + 1) * LANES] = _norm_rope(x, g_ref[0], cos_ref[0], sin_ref[0]).astype(o_ref.dtype)


def diff_prep(proj, gains, cos_t, sin_t):
    bsz, seq, _ = proj.shape
    tm = min(POST_TM, seq)
    return pl.pallas_call(
        _diff_prep_body,
        grid=(2, bsz, seq // tm),
        in_specs=[
            pl.BlockSpec((1, tm, D_INNER), lambda w, b, s: (b, s, w)),
            pl.BlockSpec((1, 1, LANES), lambda w, b, s: (w, 0, 0)),
            pl.BlockSpec((1, tm, LANES), lambda w, b, s: (b, s, 0)),
            pl.BlockSpec((1, tm, LANES), lambda w, b, s: (b, s, 0)),
        ],
        out_specs=pl.BlockSpec((1, 1, tm, D_INNER), lambda w, b, s: (w, b, s, 0)),
        out_shape=jax.ShapeDtypeStruct((2, bsz, seq, D_INNER), BF16),
        compiler_params=_cparams(("parallel", "parallel", "parallel")),
        name="diff_prep",
    )(proj, gains, cos_t, sin_t)


def _diff_epilogue(on, lam, sg, z, scale_out):
    r = on.shape[0] // 2
    o = on[:r] - lam * on[r:]
    o = o * lax.rsqrt(jnp.mean(o * o, axis=-1, keepdims=True) + EPS) * sg * scale_out
    return o * _silu(z.astype(F32))


def _diff_attn_online_body(q_ref, k_ref, v_ref, z_ref, lam_ref, sg_ref, o_ref, *, tq, scale_out):
    qi = pl.program_id(2)
    q = q_ref[0, 0]
    lane = lax.broadcasted_iota(jnp.int32, (tq, LANES), 1)
    zero = jnp.zeros_like(q)
    qq = jnp.concatenate([jnp.where(lane < DIFF_HALF, q, zero), jnp.where(lane >= DIFF_HALF, q, zero)], axis=0)

    def block(j, carry, masked):
        m, l, acc = carry
        k0 = pl.multiple_of(j * tq, tq)
        kb = k_ref[0, 0, pl.ds(k0, tq), :]
        vb = v_ref[0, pl.ds(k0, tq), :]
        s = lax.dot_general(qq, kb, (((1,), (1,)), ((), ())), preferred_element_type=F32)
        if masked:
            r = lax.broadcasted_iota(jnp.int32, (2 * tq, tq), 0)
            r = jnp.where(r >= tq, r - tq, r)
            cidx = lax.broadcasted_iota(jnp.int32, (2 * tq, tq), 1)
            s = jnp.where(cidx <= r, s, -jnp.inf)
        m_new = jnp.maximum(m, jnp.max(s, axis=-1, keepdims=True))
        alpha = jnp.exp2(m - m_new)
        p = jnp.exp2(s - m_new)
        l = alpha * l + jnp.sum(p, axis=-1, keepdims=True)
        acc = alpha * acc + jnp.dot(p.astype(BF16), vb, preferred_element_type=F32)
        return m_new, l, acc

    init = (jnp.full((2 * tq, 1), -jnp.inf, F32), jnp.zeros((2 * tq, 1), F32), jnp.zeros((2 * tq, LANES), F32))
    carry = lax.fori_loop(0, qi, lambda j, cr: block(j, cr, False), init)
    m, l, acc = block(qi, carry, True)
    o_ref[0] = _diff_epilogue(acc / l, lam_ref[...], sg_ref[...], z_ref[0], scale_out).astype(o_ref.dtype)


def _diff_attn_direct_body(qraw_ref, kraw_ref, v_ref, z_ref, g_ref, cos_ref, sin_ref, lam_ref, sg_ref, o_ref,
                           q_ref, k_ref, *, seq, scale_out):
    tq, hq = DIFF_TQ, DIFF_TQ // 2
    for i in range(seq // tq):
        rows = pl.ds(i * tq, tq)
        cos, sin = cos_ref[0, rows, :], sin_ref[0, rows, :]
        q_ref[0, 0, rows, :] = _norm_rope(qraw_ref[0, rows, :].astype(F32), g_ref[0], cos, sin).astype(BF16)
        k_ref[0, 0, rows, :] = _norm_rope(kraw_ref[0, rows, :].astype(F32), g_ref[1], cos, sin).astype(BF16)
    lane = lax.broadcasted_iota(jnp.int32, (tq, LANES), 1)
    tri = (lax.broadcasted_iota(jnp.int32, (2 * hq, hq), 1)
           <= lax.broadcasted_iota(jnp.int32, (2 * hq, hq), 0) % hq)
    nt = (((1,), (1,)), ((), ()))

    def lanesum(p):
        acc = p[:, :LANES]
        for c in range(1, p.shape[1] // LANES):
            acc = acc + p[:, c * LANES:(c + 1) * LANES]
        return acc

    def qtile(i):
        r0 = i * tq
        q = q_ref[0, 0, pl.ds(r0, tq), :]
        zero = jnp.zeros_like(q)
        q0 = jnp.where(lane < DIFF_HALF, q, zero)
        q1 = jnp.where(lane >= DIFF_HALF, q, zero)
        qq = jnp.concatenate([q0[:hq], q1[:hq], q0[hq:], q1[hq:]], axis=0)

        def offdiag(j, carry):
            acc, l = carry
            c0 = j * tq
            p = jnp.exp2(lax.dot_general(qq, k_ref[0, 0, pl.ds(c0, tq), :], nt, preferred_element_type=F32))
            return (acc + jnp.dot(p.astype(BF16), v_ref[0, pl.ds(c0, tq), :], preferred_element_type=F32),
                    l + lanesum(p))

        carry = (jnp.zeros((2 * tq, LANES), F32), jnp.zeros((2 * tq, LANES), F32))
        for j in range(i):
            carry = offdiag(j, carry)
        acc, l = carry
        r1 = r0 + hq
        k_a, v_a = k_ref[0, 0, pl.ds(r0, hq), :], v_ref[0, pl.ds(r0, hq), :]
        k_b, v_b = k_ref[0, 0, pl.ds(r1, hq), :], v_ref[0, pl.ds(r1, hq), :]
        qq_a, qq_b = qq[:tq], qq[tq:]
        p_aa = jnp.where(tri, jnp.exp2(lax.dot_general(qq_a, k_a, nt, preferred_element_type=F32)), 0.0)
        p_ba = jnp.exp2(lax.dot_general(qq_b, k_a, nt, preferred_element_type=F32))
        p_bb = jnp.where(tri, jnp.exp2(lax.dot_general(qq_b, k_b, nt, preferred_element_type=F32)), 0.0)
        acc_a = acc[:tq] + jnp.dot(p_aa.astype(BF16), v_a, preferred_element_type=F32)
        acc_b = (acc[tq:] + jnp.dot(p_ba.astype(BF16), v_a, preferred_element_type=F32)
                 + jnp.dot(p_bb.astype(BF16), v_b, preferred_element_type=F32))
        l_a = jnp.sum(l[:tq] + lanesum(p_aa), axis=-1, keepdims=True)
        l_b = jnp.sum(l[tq:] + lanesum(p_ba) + lanesum(p_bb), axis=-1, keepdims=True)
        o_ref[0, pl.ds(r0, hq), :] = _diff_epilogue(
            acc_a / l_a, lam_ref[...], sg_ref[...], z_ref[0, pl.ds(r0, hq), :], scale_out).astype(o_ref.dtype)
        o_ref[0, pl.ds(r1, hq), :] = _diff_epilogue(
            acc_b / l_b, lam_ref[...], sg_ref[...], z_ref[0, pl.ds(r1, hq), :], scale_out).astype(o_ref.dtype)

    for i in range(seq // tq):
        qtile(i)


def diff_attn(proj, gains, cos_t, sin_t, lam_vec, subln_g, lam_init, score_bound_log2):
    bsz, seq, _ = proj.shape
    koff = D_INNER // LANES
    voff = 2 * D_INNER // LANES
    zoff = 3 * D_INNER // LANES
    sg = subln_g.reshape(1, LANES).astype(F32)
    vec = pl.BlockSpec((1, LANES), lambda *_: (0, 0))

    def online(proj, gains, cos_t, sin_t, lam_vec, sg):
        qk = diff_prep(proj, gains, cos_t, sin_t)
        tq = min(DIFF_TQ, seq)
        return pl.pallas_call(
            functools.partial(_diff_attn_online_body, tq=tq, scale_out=1.0 - lam_init),
            grid=(bsz, DIFF_HEADS, seq // tq),
            in_specs=[
                pl.BlockSpec((1, 1, tq, LANES), lambda b, h, i: (0, b, i, h)),
                pl.BlockSpec((1, 1, seq, LANES), lambda b, h, i: (1, b, 0, h)),
                pl.BlockSpec((1, seq, LANES), lambda b, h, i: (b, 0, voff + h)),
                pl.BlockSpec((1, tq, LANES), lambda b, h, i: (b, i, zoff + h)),
                vec, vec,
            ],
            out_specs=pl.BlockSpec((1, tq, LANES), lambda b, h, i: (b, i, h)),
            out_shape=jax.ShapeDtypeStruct((bsz, seq, D_INNER), BF16),
            compiler_params=_cparams(("parallel", "parallel", "arbitrary")),
            name="diff_attn_online",
        )(qk, qk, proj, proj, lam_vec, sg)

    def direct(proj, gains, cos_t, sin_t, lam_vec, sg):
        return pl.pallas_call(
            functools.partial(_diff_attn_direct_body, seq=seq, scale_out=1.0 - lam_init),
            grid=(bsz, DIFF_HEADS),
            in_specs=[
                pl.BlockSpec((1, seq, LANES), lambda b, h: (b, 0, h)),
                pl.BlockSpec((1, seq, LANES), lambda b, h: (b, 0, koff + h)),
                pl.BlockSpec((1, seq, LANES), lambda b, h: (b, 0, voff + h)),
                pl.BlockSpec((1, seq, LANES), lambda b, h: (b, 0, zoff + h)),
                pl.BlockSpec((2, 1, LANES), lambda b, h: (0, 0, 0)),
                pl.BlockSpec((1, seq, LANES), lambda b, h: (b, 0, 0)),
                pl.BlockSpec((1, seq, LANES), lambda b, h: (b, 0, 0)),
                vec, vec,
            ],
            out_specs=pl.BlockSpec((1, seq, LANES), lambda b, h: (b, 0, h)),
            out_shape=jax.ShapeDtypeStruct((bsz, seq, D_INNER), BF16),
            scratch_shapes=[pltpu.VMEM((1, 1, seq, LANES), BF16), pltpu.VMEM((1, 1, seq, LANES), BF16)],
            compiler_params=_cparams(("parallel", "parallel")),
            name="diff_attn_direct",
        )(proj, proj, proj, proj, gains, cos_t, sin_t, lam_vec, sg)

    operands = (proj, gains, cos_t, sin_t, lam_vec, sg)
    if seq % DIFF_TQ != 0:
        return online(*operands)
    return lax.cond(score_bound_log2 <= DIFF_DIRECT_MAX_LOG2, direct, online, *operands)


def _ssd_body(xbc_ref, z_ref, dt_ref, cw_ref, cb_ref, dtb_ref, alog_ref, dsk_ref, ng_ref, e64_ref, e512_ref,
              o_ref, xs_ref, st_ref, *, ts):
    L = SSD_CHUNK
    gw = SSD_HPG * SSD_HEAD_DIM
    pad = 8

    @pl.when(pl.program_id(1) == 0)
    def _():
        st_ref[...] = jnp.zeros_like(st_ref)
        xs_ref[0:pad, :] = jnp.zeros((pad, SSD_CONV_CH), F32)

    @pl.when(pl.program_id(1) > 0)
    def _():
        xs_ref[0:pad, :] = xs_ref[ts:ts + pad, :]

    xs_ref[pad:pad + ts, :] = xbc_ref[0].astype(F32)
    cw = cw_ref[...]
    xt = xs_ref[0:pad + ts, :]
    u = xt * cw[0:1, :]
    for w in range(1, SSD_CONV):
        u = pltpu.roll(u, 1, 0) + xt * cw[w:w + 1, :]
    xc_all = _silu(u[pad:, :] + cb_ref[...])

    tri3 = _tri3(L)
    lane_grp = lax.broadcasted_iota(jnp.int32, (L, LANES), 1) // SSD_HEADS
    row4 = lax.broadcasted_iota(jnp.int32, (L, SSD_HPG * L), 0)
    col4 = lax.broadcasted_iota(jnp.int32, (L, SSD_HPG * L), 1) % L
    causal4 = col4 <= row4
    br = lax.broadcasted_iota(jnp.int32, (SSD_HPG * L, gw), 0) // L
    bc = lax.broadcasted_iota(jnp.int32, (SSD_HPG * L, gw), 1) // SSD_HEAD_DIM
    blockdiag = br == bc
    a_neg = -jnp.exp(alog_ref[...]) * math.log2(math.e)

    for c in range(ts // L):
        rs = slice(c * L, (c + 1) * L)
        xc = xc_all[rs, :]
        xr = dt_ref[0, rs, :] + dtb_ref[...]
        dt = jnp.maximum(xr, 0.0) + jnp.log(1.0 + jnp.exp(-jnp.abs(xr)))
        cum = _cumsum_rows(tri3, dt * a_neg)
        cum_p = _pack_split3(cum, lane_grp)
        cum64 = jnp.dot(cum_p, e64_ref[...], preferred_element_type=F32)
        dt64 = jnp.dot(_pack_split3(dt, lane_grp), e64_ref[...], preferred_element_type=F32)
        cum512 = jnp.dot(cum_p, e512_ref[...], preferred_element_type=F32)
        cum_t = cum.T
        last64 = cum64[L - 1:L, :]
        ys = []
        for g in range(SSD_GROUPS):
            xg = xc[:, g * gw:(g + 1) * gw]
            bg = xc[:, D_INNER + g * SSD_STATE:D_INNER + (g + 1) * SSD_STATE]
            cg = xc[:, D_INNER + SSD_GROUPS * SSD_STATE + g * SSD_STATE:
                    D_INNER + SSD_GROUPS * SSD_STATE + (g + 1) * SSD_STATE]
            cgb = cg.astype(BF16)
            cb = lax.dot_general(cgb, bg.astype(BF16), (((1,), (1,)), ((), ())), preferred_element_type=F32)
            ce = cum512[:, g * SSD_HPG * L:(g + 1) * SSD_HPG * L]
            crow = jnp.concatenate([cum_t[g * SSD_HPG + hh:g * SSD_HPG + hh + 1, :] for hh in range(SSD_HPG)],
                                   axis=1)
            m = jnp.where(causal4, jnp.exp2(jnp.minimum(ce - crow, 0.0)), 0.0) * jnp.concatenate([cb] * SSD_HPG, axis=1)
            cg64 = cum64[:, g * gw:(g + 1) * gw]
            dg64 = dt64[:, g * gw:(g + 1) * gw]
            xdt = xg * dg64
            rhs = jnp.where(blockdiag, jnp.concatenate([xdt] * SSD_HPG, axis=0), 0.0)
            y = jnp.dot(m.astype(BF16), rhs.astype(BF16), preferred_element_type=F32)
            state = st_ref[g]
            y = y + jnp.dot(cgb, state.astype(BF16), preferred_element_type=F32) * jnp.exp2(cg64)
            lg = last64[:, g * gw:(g + 1) * gw]
            xw = xg * (jnp.exp2(lg - cg64) * dg64)
            st_ref[g] = state * jnp.exp2(lg) + jnp.dot(bg.T.astype(BF16), xw.astype(BF16), preferred_element_type=F32)
            ys.append(y + dsk_ref[:, g * gw:(g + 1) * gw] * xg)
        y = jnp.concatenate(ys, axis=1) * _silu(z_ref[0, rs, :].astype(F32))
        y = y * lax.rsqrt(jnp.mean(y * y, axis=-1, keepdims=True) + EPS) * ng_ref[...]
        o_ref[0, rs, :] = y.astype(o_ref.dtype)


def _ssd_expanders():
    e64 = np.zeros((LANES, D_INNER), np.float32)
    e512 = np.zeros((LANES, SSD_HEADS * SSD_CHUNK), np.float32)
    for rep in range(SSD_REPS):
        for h in range(SSD_HEADS):
            e64[rep * SSD_HEADS + h, h * SSD_HEAD_DIM:(h + 1) * SSD_HEAD_DIM] = 1.0
            e512[rep * SSD_HEADS + h, h * SSD_CHUNK:(h + 1) * SSD_CHUNK] = 1.0
    return jnp.asarray(e64, BF16), jnp.asarray(e512, BF16)


def ssd_mixer(proj, dt_aux, conv_w, conv_b, dt_bias, a_log, d_skip, norm_g):
    bsz, seq, _ = proj.shape
    ts = min(SSD_TS, seq)
    e64, e512 = _ssd_expanders()

    def pad128(v):
        return jnp.zeros((1, LANES), F32).at[0, :SSD_REPS * SSD_HEADS].set(jnp.tile(v.astype(F32), SSD_REPS))

    dsk = jnp.repeat(d_skip.astype(F32), SSD_HEAD_DIM).reshape(1, D_INNER)
    full = lambda shape: pl.BlockSpec(shape, lambda b, s: (0,) * len(shape))
    return pl.pallas_call(
        functools.partial(_ssd_body, ts=ts),
        grid=(bsz, seq // ts),
        in_specs=[
            pl.BlockSpec((1, ts, SSD_CONV_CH), lambda b, s: (b, s, 0)),
            pl.BlockSpec((1, ts, D_INNER), lambda b, s: (b, s, 2)),
            pl.BlockSpec((1, ts, LANES), lambda b, s: (b, s, 0)),
            full((SSD_CONV, SSD_CONV_CH)),
            full((1, SSD_CONV_CH)),
            full((1, LANES)),
            full((1, LANES)),
            full((1, D_INNER)),
            full((1, D_INNER)),
            full((LANES, D_INNER)),
            full((LANES, SSD_HEADS * SSD_CHUNK)),
        ],
        out_specs=pl.BlockSpec((1, ts, D_INNER), lambda b, s: (b, s, 0)),
        out_shape=jax.ShapeDtypeStruct((bsz, seq, D_INNER), BF16),
        scratch_shapes=[pltpu.VMEM((ts + 16, SSD_CONV_CH), F32),
                        pltpu.VMEM((SSD_GROUPS, SSD_STATE, SSD_HPG * SSD_HEAD_DIM), F32)],
        compiler_params=_cparams(("parallel", "arbitrary")),
        name="ssd_mixer",
    )(proj, proj, dt_aux, conv_w.astype(F32), conv_b.reshape(1, -1).astype(F32), pad128(dt_bias), pad128(a_log),
      dsk, norm_g.reshape(1, -1).astype(F32), e64, e512)


def _pad_cols(w, width=LANES):
    return jnp.zeros((w.shape[0], width), F32).at[:, :w.shape[1]].set(w.astype(F32))


def s5_layer(x, norm_g, w_in, lam_re, lam_im, log_step, b_re, b_im, c_re, c_im, d_skip, w_glu):
    bsz, seq, d = x.shape
    proj = in_proj(x.reshape(bsz * seq, d), norm_g, w_in.astype(BF16)).reshape(bsz, seq, -1)
    y = s5_slab(proj, *s5_slab_params(lam_re, lam_im, log_step, b_re, b_im, c_re, c_im))
    branch = s5_post(y, proj, d_skip, w_glu.astype(BF16))
    return branch, proj, (2 * D_INNER) // D_X, (2 * D_INNER) // D_X + 1


def gla_layer(x, norm_g, w_in, w_gk2, b_gk2, gla_norm_g):
    bsz, seq, d = x.shape
    nqk = GLA_HEADS * GLA_DK
    c0 = 2 * nqk + D_INNER
    w_main = jnp.concatenate([w_in[:, :c0], w_in[:, c0 + GLA_RANK:]], axis=1).astype(BF16)
    proj, gk_aux = in_proj(x.reshape(bsz * seq, d), norm_g, w_main, _pad_cols(w_in[:, c0:c0 + GLA_RANK]))
    proj = proj.reshape(bsz, seq, -1)
    branch = gla_mixer(proj, gk_aux.reshape(bsz, seq, LANES), w_gk2, b_gk2, gla_norm_g)
    zx_col = (c0 + D_INNER) // D_X
    return branch, proj, zx_col, zx_col + 1


def diff_layer(x, norm_g, w_in, cos_t, sin_t, q_g, k_g, lq1, lk1, lq2, lk2, subln_g, lam_init):
    bsz, seq, d = x.shape
    proj = in_proj(x.reshape(bsz * seq, d), norm_g, w_in.astype(BF16)).reshape(bsz, seq, -1)
    q_scale = (DIFF_HALF ** -0.5) * math.log2(math.e)
    gains = jnp.stack([jnp.tile(q_g.astype(F32), LANES // DIFF_HALF) * q_scale,
                       jnp.tile(k_g.astype(F32), LANES // DIFF_HALF)]).reshape(2, 1, LANES)
    score_bound_log2 = (BF16_ROUNDING_SLACK * DIFF_HALF * q_scale
                        * jnp.max(jnp.abs(q_g.astype(F32))) * jnp.max(jnp.abs(k_g.astype(F32))))
    lam = (jnp.exp(jnp.sum(lq1.astype(F32) * lk1.astype(F32)))
           - jnp.exp(jnp.sum(lq2.astype(F32) * lk2.astype(F32))) + lam_init)
    lam_vec = jnp.full((1, LANES), lam, F32)
    branch = diff_attn(proj, gains, cos_t, sin_t, lam_vec, subln_g, lam_init, score_bound_log2)
    zx_col = (4 * D_INNER) // D_X
    return branch, proj, zx_col, zx_col + 1


def ssd_layer(x, norm_g, w_in, conv_w, conv_b, dt_bias, a_log, d_skip, ssd_norm_g):
    bsz, seq, d = x.shape
    c0 = SSD_CONV_CH
    w_main = jnp.concatenate([w_in[:, :c0], w_in[:, c0 + SSD_HEADS:]], axis=1).astype(BF16)
    w_dt = jnp.tile(w_in[:, c0:c0 + SSD_HEADS], (1, SSD_REPS))
    proj, dt_aux = in_proj(x.reshape(bsz * seq, d), norm_g, w_main, _pad_cols(w_dt))
    proj = proj.reshape(bsz, seq, -1)
    branch = ssd_mixer(proj, dt_aux.reshape(bsz, seq, LANES), conv_w, conv_b, dt_bias, a_log, d_skip, ssd_norm_g)
    zx_col = (c0 + D_INNER) // D_X
    return branch, proj, zx_col, zx_col + 1


def rope_tables(positions):
    inv = ROPE_THETA ** (-jnp.arange(0, DIFF_HALF, 2, dtype=F32) / DIFF_HALF)
    ang = positions.astype(F32)[..., None] * inv
    reps = LANES // (DIFF_HALF // 2)
    return jnp.tile(jnp.cos(ang), (1, 1, reps)), jnp.tile(jnp.sin(ang), (1, 1, reps))


def kernel(x, mem, positions, norm_g, w_out, mem_norm_g, w_mem_kv, xq_g, xk_g, s5_w_in, s5_lam_re, s5_lam_im, s5_log_step, s5_b_re, s5_b_im, s5_c_re, s5_c_im, s5_d, s5_w_glu, gla_w_in, gla_w_gk2, gla_b_gk2, gla_norm_g, diff_w_in, diff_q_g, diff_k_g, diff_lq1, diff_lk1, diff_lq2, diff_lk2, diff_subln_g, ssd_w_in, ssd_conv_w, ssd_conv_b, ssd_dt_bias, ssd_a_log, ssd_d, ssd_norm_g):
    depth = norm_g.shape[0]
    k_all, v_all = mem_kv(mem, mem_norm_g, w_mem_kv, xk_g)
    cos_t, sin_t = rope_tables(positions)
    w_out_b = w_out.astype(BF16)
    for i in range(depth):
        kind, j = i % 4, i // 4
        if kind == 0:
            branch, proj, zc, qc = s5_layer(x, norm_g[i], s5_w_in[j], s5_lam_re[j], s5_lam_im[j], s5_log_step[j],
                                            s5_b_re[j], s5_b_im[j], s5_c_re[j], s5_c_im[j], s5_d[j], s5_w_glu[j])
        elif kind == 1:
            branch, proj, zc, qc = gla_layer(x, norm_g[i], gla_w_in[j], gla_w_gk2[j], gla_b_gk2[j], gla_norm_g[j])
        elif kind == 2:
            lam_init = 0.8 - 0.6 * math.exp(-0.3 * i)
            branch, proj, zc, qc = diff_layer(x, norm_g[i], diff_w_in[j], cos_t, sin_t, diff_q_g[j], diff_k_g[j],
                                              diff_lq1[j], diff_lk1[j], diff_lq2[j], diff_lk2[j], diff_subln_g[j],
                                              lam_init)
        else:
            branch, proj, zc, qc = ssd_layer(x, norm_g[i], ssd_w_in[j], ssd_conv_w[j], ssd_conv_b[j], ssd_dt_bias[j],
                                             ssd_a_log[j], ssd_d[j], ssd_norm_g[j])
        x = out_stage(branch, proj, zc, qc, k_all[i], v_all[i], x, w_out_b[i], xq_g[i])
    return x
```

```python
import functools
import math

import numpy as np
import jax
import jax.numpy as jnp
from jax import lax
from jax.experimental import pallas as pl
from jax.experimental.pallas import tpu as pltpu

F32 = jnp.float32
BF16 = jnp.bfloat16

D_MODEL = 1024
D_INNER = 2048
EPS = 1e-6
MEM_LEN = 256
X_HEADS = 4
X_HEAD_DIM = 128
D_X = X_HEADS * X_HEAD_DIM
D_GATE = D_INNER + D_X
S5_GROUP = 16
S5_GROUPS = D_INNER // S5_GROUP
S5_STATE = 64
GLA_HEADS = 4
GLA_DK = 128
GLA_DV = 512
GLA_RANK = 16
GLA_TAU = 16.0
DIFF_HEADS = 16
DIFF_HALF = 64
DIFF_VDIM = 128
ROPE_THETA = 10000.0
SSD_HEAD_DIM = 64
SSD_HEADS = 32
SSD_GROUPS = 8
SSD_HPG = 4
SSD_STATE = 128
SSD_CONV = 4
SSD_CONV_CH = D_INNER + 2 * SSD_GROUPS * SSD_STATE

LANES = 128
MXU_WIDTH = 256
VMEM_LIMIT = 48 * 1024 * 1024

PROJ_TM = 1024
PROJ_TN_MAX = 2560
OUT_TM = 1024
POST_TM = 512
S5_BT = 16
S5_T = 16
S5_TS = 512
GLA_TS = 1024
GLA_CHUNK = 128
GLA_SUB = 16
GLA_SAFE_DECAY = 40.0
DIFF_TQ = 512
DIFF_DIRECT_MAX_LOG2 = 64.0
BF16_ROUNDING_SLACK = 1.02
SSD_TS = 512
SSD_CHUNK = 128
SSD_REPS = 3


def _cparams(sem):
    return pltpu.CompilerParams(dimension_semantics=sem, vmem_limit_bytes=VMEM_LIMIT)


def _sigmoid(x):
    return 1.0 / (1.0 + jnp.exp(-x))


def _silu(x):
    return x * _sigmoid(x)


def _split3(x):
    x1 = x.astype(BF16)
    r1 = x - x1.astype(F32)
    x2 = r1.astype(BF16)
    x3 = (r1 - x2.astype(F32)).astype(BF16)
    return x1, x2, x3


def _cumsum_rows(tri3, x):
    return jnp.dot(tri3, jnp.concatenate(_split3(x), axis=0), preferred_element_type=F32)


def _pack_split3(x, lane_grp):
    x1, x2, x3 = _split3(x)
    zero = jnp.zeros_like(x1)
    return jnp.where(lane_grp == 0, x1, jnp.where(lane_grp == 1, x2, jnp.where(lane_grp == 2, x3, zero)))


def _tri3(n):
    row = lax.broadcasted_iota(jnp.int32, (n, n), 0)
    col = lax.broadcasted_iota(jnp.int32, (n, n), 1)
    tri = (col <= row).astype(BF16)
    return jnp.concatenate([tri, tri, tri], axis=1)


def _in_proj_body(x_ref, g_ref, w_ref, *rest, has_aux):
    if has_aux:
        waux_ref, o_ref, oaux_ref, h_ref = rest
    else:
        o_ref, h_ref = rest

    @pl.when(pl.program_id(1) == 0)
    def _():
        x = x_ref[...]
        h = x * lax.rsqrt(jnp.mean(x * x, axis=-1, keepdims=True) + EPS) * g_ref[...]
        hb = h.astype(BF16)
        h_ref[...] = hb
        if has_aux:
            oaux_ref[...] = jnp.dot(hb, waux_ref[...], preferred_element_type=F32)

    o_ref[...] = jnp.dot(h_ref[...], w_ref[...], preferred_element_type=F32).astype(o_ref.dtype)


def in_proj(x2d, g, w_bf16, w_aux=None):
    n, d = x2d.shape
    nc = w_bf16.shape[1]
    tm = min(PROJ_TM, n)
    tn = max(t for t in range(MXU_WIDTH, PROJ_TN_MAX + 1, MXU_WIDTH) if nc % t == 0)
    assert n % tm == 0
    has_aux = w_aux is not None
    in_specs = [
        pl.BlockSpec((tm, d), lambda i, j: (i, 0)),
        pl.BlockSpec((1, d), lambda i, j: (0, 0)),
        pl.BlockSpec((d, tn), lambda i, j: (0, j)),
    ]
    args = [x2d, g.reshape(1, d).astype(F32), w_bf16]
    out_shape = [jax.ShapeDtypeStruct((n, nc), BF16)]
    out_specs = [pl.BlockSpec((tm, tn), lambda i, j: (i, j))]
    if has_aux:
        in_specs.append(pl.BlockSpec((d, LANES), lambda i, j: (0, 0)))
        args.append(w_aux.astype(BF16))
        out_shape.append(jax.ShapeDtypeStruct((n, LANES), F32))
        out_specs.append(pl.BlockSpec((tm, LANES), lambda i, j: (i, 0)))
    res = pl.pallas_call(
        functools.partial(_in_proj_body, has_aux=has_aux),
        grid=(n // tm, nc // tn),
        in_specs=in_specs,
        out_specs=out_specs,
        out_shape=out_shape,
        scratch_shapes=[pltpu.VMEM((tm, d), BF16)],
        compiler_params=_cparams(("parallel", "arbitrary")),
        name="in_proj_aux" if has_aux else "in_proj",
    )(*args)
    return (res[0], res[1]) if has_aux else res[0]


def _mem_kv_body(mem_ref, mg_ref, w_ref, kg_ref, k_ref, v_ref):
    m = mem_ref[0]
    mn = m * lax.rsqrt(jnp.mean(m * m, axis=-1, keepdims=True) + EPS) * mg_ref[...]
    kv = jnp.dot(mn.astype(BF16), w_ref[0], preferred_element_type=F32)
    kg = kg_ref[0]
    for h in range(X_HEADS):
        kh = kv[:, h * X_HEAD_DIM:(h + 1) * X_HEAD_DIM]
        kh = kh * lax.rsqrt(jnp.mean(kh * kh, axis=-1, keepdims=True) + EPS) * kg
        k_ref[0, 0, :, h * X_HEAD_DIM:(h + 1) * X_HEAD_DIM] = kh.astype(BF16)
    v_ref[0, 0] = kv[:, D_X:].astype(BF16)


def mem_kv(mem, mem_norm_g, w_mem_kv, xk_g):
    bsz, mlen, d = mem.shape
    depth = w_mem_kv.shape[0]
    return pl.pallas_call(
        _mem_kv_body,
        grid=(depth, bsz),
        in_specs=[
            pl.BlockSpec((1, mlen, d), lambda l, b: (b, 0, 0)),
            pl.BlockSpec((1, d), lambda l, b: (0, 0)),
            pl.BlockSpec((1, d, 2 * D_X), lambda l, b: (l, 0, 0)),
            pl.BlockSpec((1, 1, X_HEAD_DIM), lambda l, b: (l, 0, 0)),
        ],
        out_specs=[
            pl.BlockSpec((1, 1, mlen, D_X), lambda l, b: (l, b, 0, 0)),
            pl.BlockSpec((1, 1, mlen, D_X), lambda l, b: (l, b, 0, 0)),
        ],
        out_shape=[jax.ShapeDtypeStruct((depth, bsz, mlen, D_X), BF16)] * 2,
        compiler_params=_cparams(("parallel", "parallel")),
        name="mem_kv",
    )(mem, mem_norm_g.reshape(1, d).astype(F32), w_mem_kv.astype(BF16),
      xk_g.reshape(depth, 1, X_HEAD_DIM).astype(F32))


def _out_body(br_ref, zx_ref, qx_ref, k_ref, v_ref, x_ref, w_ref, qg_ref, o_ref):
    q = qx_ref[0].astype(F32)
    zx = zx_ref[0].astype(F32)
    qg = qg_ref[...]
    outs = []
    for h in range(X_HEADS):
        sl = slice(h * X_HEAD_DIM, (h + 1) * X_HEAD_DIM)
        qh = q[:, sl]
        qn = qh * lax.rsqrt(jnp.mean(qh * qh, axis=-1, keepdims=True) + EPS) * qg
        s = lax.dot_general(qn.astype(BF16), k_ref[0, :, sl], (((1,), (1,)), ((), ())),
                            preferred_element_type=F32) * (X_HEAD_DIM ** -0.5)
        p = jnp.exp(s - jnp.max(s, axis=-1, keepdims=True))
        l = jnp.sum(p, axis=-1, keepdims=True)
        oh = jnp.dot(p.astype(BF16), v_ref[0, :, sl], preferred_element_type=F32) / l
        outs.append((oh * _silu(zx[:, sl])).astype(BF16))
    mem_out = jnp.concatenate(outs, axis=-1)
    acc = jnp.dot(br_ref[0], w_ref[:D_INNER, :], preferred_element_type=F32)
    acc = acc + jnp.dot(mem_out, w_ref[D_INNER:, :], preferred_element_type=F32)
    o_ref[0] = x_ref[0] + acc


def out_stage(branch, proj, zx_col, qx_col, k_l, v_l, x, w_out_bf16, xq_g):
    bsz, seq, _ = branch.shape
    tm = min(OUT_TM, seq)
    return pl.pallas_call(
        _out_body,
        grid=(bsz, seq // tm),
        in_specs=[
            pl.BlockSpec((1, tm, D_INNER), lambda b, s: (b, s, 0)),
            pl.BlockSpec((1, tm, D_X), lambda b, s: (b, s, zx_col)),
            pl.BlockSpec((1, tm, D_X), lambda b, s: (b, s, qx_col)),
            pl.BlockSpec((1, MEM_LEN, D_X), lambda b, s: (b, 0, 0)),
            pl.BlockSpec((1, MEM_LEN, D_X), lambda b, s: (b, 0, 0)),
            pl.BlockSpec((1, tm, D_MODEL), lambda b, s: (b, s, 0)),
            pl.BlockSpec((D_GATE, D_MODEL), lambda b, s: (0, 0)),
            pl.BlockSpec((1, X_HEAD_DIM), lambda b, s: (0, 0)),
        ],
        out_specs=pl.BlockSpec((1, tm, D_MODEL), lambda b, s: (b, s, 0)),
        out_shape=jax.ShapeDtypeStruct((bsz, seq, D_MODEL), F32),
        compiler_params=_cparams(("parallel", "parallel")),
        name="out_stage",
    )(branch, proj, proj, k_l, v_l, x, w_out_bf16, xq_g.reshape(1, X_HEAD_DIM).astype(F32))


def _s5_slab_body(u_ref, bd_ref, mc_ref, nc_ref, a_ref, y_ref, w_ref, m_ref, n_ref, xs_ref, sp_ref, st_ref, *, nc, bt):
    T = S5_T
    gps = LANES // S5_GROUP
    half = S5_STATE * gps

    @pl.when((pl.program_id(1) == 0) & (pl.program_id(2) == 0))
    def _():
        for s in range(T):
            for t in range(max(s - 1, 0), T):
                blk = bd_ref[0, t - s] if t >= s else jnp.zeros((LANES, LANES), BF16)
                w_ref[s * LANES:(s + 1) * LANES, t * LANES:(t + 1) * LANES] = blk
        mrow = lax.broadcasted_iota(jnp.int32, (T * LANES, half), 0) // S5_GROUP % gps
        mcol = lax.broadcasted_iota(jnp.int32, (T * LANES, half), 1) // S5_STATE
        nrow = lax.broadcasted_iota(jnp.int32, (half, T * LANES), 0) // S5_STATE
        ncol = lax.broadcasted_iota(jnp.int32, (half, T * LANES), 1) // S5_GROUP % gps
        for part in range(2):
            mt = jnp.concatenate([mc_ref[0, part]] * (half // LANES), axis=1)
            m_ref[:, part * half:(part + 1) * half] = jnp.where(mrow == mcol, mt, jnp.zeros_like(mt))
            nt = jnp.concatenate([nc_ref[0, part]] * gps, axis=0)
            n_ref[part * half:(part + 1) * half, :] = jnp.where(nrow == ncol, nt, jnp.zeros_like(nt))

    @pl.when(pl.program_id(2) == 0)
    def _():
        st_ref[...] = jnp.zeros_like(st_ref)

    v4 = pltpu.einshape("bcsl->csbl", u_ref[...].reshape(bt, nc, T, LANES))
    v = jnp.concatenate([v4[:, s].reshape(nc * bt, LANES) for s in range(T)], axis=1)
    xs_ref[...] = jnp.dot(v, m_ref[...], preferred_element_type=F32)
    a = a_ref[0]
    a_re, a_im = a[:, :half], a[:, half:]

    def step(c, carry):
        sr, si = carry
        r0 = pl.multiple_of(c * bt, bt)
        sp_ref[pl.ds(r0, bt), :half] = sr
        sp_ref[pl.ds(r0, bt), half:] = si
        x = xs_ref[pl.ds(r0, bt), :]
        return a_re * sr - a_im * si + x[:, :half], a_re * si + a_im * sr + x[:, half:]

    st = st_ref[...]
    sr, si = lax.fori_loop(0, nc, step, (st[:, :half], st[:, half:]), unroll=4)
    st_ref[:, :half] = sr
    st_ref[:, half:] = si
    y_state = jnp.dot(sp_ref[...].astype(BF16), n_ref[...], preferred_element_type=F32)
    cols = []
    for tp in range(T // 2):
        kk = (2 * tp + 2) * LANES
        cols.append(jnp.dot(v[:, :kk], w_ref[:kk, tp * 2 * LANES:(tp + 1) * 2 * LANES], preferred_element_type=F32))
    y = (y_state + jnp.concatenate(cols, axis=1)).astype(BF16)
    y4 = jnp.stack([y[:, t * LANES:(t + 1) * LANES].reshape(nc, bt, LANES) for t in range(T)], axis=1)
    y_ref[...] = pltpu.einshape("ctbl->bctl", y4).reshape(bt, nc * T, LANES)


def s5_slab(proj, bd, m_compact, n_compact, a_slab):
    bsz, seq, _ = proj.shape
    bt = S5_BT
    ts = min(S5_TS, seq)
    nslab = D_INNER // LANES
    nstate = 2 * S5_STATE * (LANES // S5_GROUP)
    assert bsz % bt == 0 and seq % ts == 0
    return pl.pallas_call(
        functools.partial(_s5_slab_body, nc=ts // S5_T, bt=bt),
        grid=(nslab, bsz // bt, seq // ts),
        in_specs=[
            pl.BlockSpec((bt, ts, LANES), lambda j, b, s: (b, s, j)),
            pl.BlockSpec((1, S5_T, LANES, LANES), lambda j, b, s: (j, 0, 0, 0)),
            pl.BlockSpec((1, 2, S5_T * LANES, LANES), lambda j, b, s: (j, 0, 0, 0)),
            pl.BlockSpec((1, 2, S5_STATE, S5_T * LANES), lambda j, b, s: (j, 0, 0, 0)),
            pl.BlockSpec((1, bt, nstate), lambda j, b, s: (j, 0, 0)),
        ],
        out_specs=pl.BlockSpec((bt, ts, LANES), lambda j, b, s: (b, s, j)),
        out_shape=jax.ShapeDtypeStruct((bsz, seq, D_INNER), BF16),
        scratch_shapes=[pltpu.VMEM((S5_T * LANES, S5_T * LANES), BF16),
                        pltpu.VMEM((S5_T * LANES, nstate), BF16),
                        pltpu.VMEM((nstate, S5_T * LANES), BF16),
                        pltpu.VMEM((ts // S5_T * bt, nstate), F32),
                        pltpu.VMEM((ts // S5_T * bt, nstate), F32),
                        pltpu.VMEM((bt, nstate), F32)],
        compiler_params=_cparams(("arbitrary", "arbitrary", "arbitrary")),
        name="s5_slab",
    )(proj, bd, m_compact, n_compact, a_slab)


def s5_slab_params(lam_re, lam_im, log_step, b_re, b_im, c_re, c_im):
    T, G, P, H = S5_T, S5_GROUPS, S5_STATE, S5_GROUP
    gps = LANES // H
    nslab = G // gps
    hp = lax.Precision.HIGHEST
    step = jnp.exp(log_step.astype(F32))[:, None]
    lam_re = lam_re.astype(F32)
    lam_im = lam_im.astype(F32)
    mag = jnp.exp(lam_re * step)
    lb_re = mag * jnp.cos(lam_im * step)
    lb_im = mag * jnp.sin(lam_im * step)
    den = lam_re * lam_re + lam_im * lam_im
    nr = lb_re - 1.0
    co_re = (nr * lam_re + lb_im * lam_im) / den
    co_im = (lb_im * lam_re - nr * lam_im) / den
    b_re = b_re.astype(F32)
    b_im = b_im.astype(F32)
    bb_re = co_re[..., None] * b_re - co_im[..., None] * b_im
    bb_im = co_re[..., None] * b_im + co_im[..., None] * b_re
    c_re = c_re.astype(F32)
    c_im = c_im.astype(F32)
    j = jnp.arange(T + 1, dtype=F32)[:, None, None]
    pmag = jnp.exp(j * (lam_re * step))
    pw_re = pmag * jnp.cos(j * (lam_im * step))
    pw_im = pmag * jnp.sin(j * (lam_im * step))
    cl_re = c_re[None] * pw_re[:, :, None, :] - c_im[None] * pw_im[:, :, None, :]
    cl_im = c_re[None] * pw_im[:, :, None, :] + c_im[None] * pw_re[:, :, None, :]
    kj = jnp.einsum('gjkp,gph->jgkh', jnp.concatenate([cl_re, -cl_im], axis=-1).transpose(1, 0, 2, 3),
                    jnp.concatenate([bb_re, bb_im], axis=1), precision=hp)
    eye = jnp.eye(gps, dtype=F32)
    bd = jnp.einsum('djgkh,gq->jdghqk', kj[:T].reshape(T, nslab, gps, H, H), eye).reshape(nslab, T, LANES, LANES)
    rev = T - 1 - jnp.arange(T)
    m_re = pw_re[rev][..., None] * bb_re[None] - pw_im[rev][..., None] * bb_im[None]
    m_im = pw_re[rev][..., None] * bb_im[None] + pw_im[rev][..., None] * bb_re[None]

    def m_rows(m):
        r = m.reshape(T, nslab, gps, P, H).transpose(1, 0, 2, 4, 3).reshape(nslab, T * LANES, P)
        return jnp.concatenate([r, r], axis=-1)

    m_compact = jnp.stack([m_rows(m_re), m_rows(m_im)], axis=1)

    def n_rows(cl):
        return cl[1:].reshape(T, nslab, gps, H, P).transpose(1, 4, 0, 2, 3).reshape(nslab, P, T * LANES)

    n_compact = jnp.stack([n_rows(cl_re), -n_rows(cl_im)], axis=1)
    a = jnp.concatenate([pw_re[T].reshape(nslab, gps * P), pw_im[T].reshape(nslab, gps * P)], axis=-1)
    a_slab = jnp.broadcast_to(a[:, None, :], (nslab, S5_BT, 2 * gps * P))
    return bd.astype(BF16), m_compact.astype(BF16), n_compact.astype(BF16), a_slab


def _s5_post_body(y_ref, u_ref, z_ref, d_ref, w_ref, o_ref):
    yv = y_ref[0].astype(F32) + d_ref[...] * u_ref[0].astype(F32)
    g = 0.5 * yv * (1.0 + jnp.tanh(math.sqrt(2.0 / math.pi) * (yv + 0.044715 * (yv * yv * yv))))
    gate = _sigmoid(jnp.dot(g.astype(BF16), w_ref[...], preferred_element_type=F32))
    o_ref[0] = (g * gate * _silu(z_ref[0].astype(F32))).astype(o_ref.dtype)


def s5_post(y, proj, d_skip, w_glu_bf16):
    bsz, seq, _ = y.shape
    tm = min(POST_TM, seq)
    return pl.pallas_call(
        _s5_post_body,
        grid=(bsz, seq // tm),
        in_specs=[
            pl.BlockSpec((1, tm, D_INNER), lambda b, s: (b, s, 0)),
            pl.BlockSpec((1, tm, D_INNER), lambda b, s: (b, s, 0)),
            pl.BlockSpec((1, tm, D_INNER), lambda b, s: (b, s, 1)),
            pl.BlockSpec((1, D_INNER), lambda b, s: (0, 0)),
            pl.BlockSpec((D_INNER, D_INNER), lambda b, s: (0, 0)),
        ],
        out_specs=pl.BlockSpec((1, tm, D_INNER), lambda b, s: (b, s, 0)),
        out_shape=jax.ShapeDtypeStruct((bsz, seq, D_INNER), BF16),
        compiler_params=_cparams(("parallel", "parallel")),
        name="s5_post",
    )(y, proj, proj, d_skip.reshape(1, D_INNER).astype(F32), w_glu_bf16)


def _gla_body(q_ref, k_ref, v_ref, z_ref, gk_ref, w2_ref, b2_ref, ng_ref, o_ref, st_ref, att_ref, *, ts):
    C, R = GLA_CHUNK, GLA_SUB
    nsub = C // R
    nt = (((1,), (1,)), ((), ()))

    @pl.when(pl.program_id(2) == 0)
    def _():
        st_ref[...] = jnp.zeros_like(st_ref)

    tri3 = _tri3(C)
    row_r = lax.broadcasted_iota(jnp.int32, (R, C), 0)
    col_r = lax.broadcasted_iota(jnp.int32, (R, C), 1)

    nchunk = ts // C
    qs_all, ks_all, bs_all, decays = [], [], [], []
    for c in range(nchunk):
        rs = slice(c * C, (c + 1) * C)
        x = jnp.dot(gk_ref[0, rs, :].astype(BF16), w2_ref[...], preferred_element_type=F32) + b2_ref[...]
        g = (jnp.minimum(x, 0.0) - jnp.log(1.0 + jnp.exp(-jnp.abs(x)))) * (1.0 / GLA_TAU)
        bs_all.append(_cumsum_rows(tri3, g))
        qs_all.append(q_ref[0, rs, :].astype(F32) * (GLA_DK ** -0.5))
        ks_all.append(k_ref[0, rs, :].astype(F32))
        decays.append(jnp.max(-jnp.sum(g.reshape(nsub, R, GLA_DK), axis=1)))
    sub_decay = functools.reduce(jnp.maximum, decays)

    def scores_matmul(c, i, cap):
        q, k, b = qs_all[c], ks_all[c], bs_all[c]
        bi = b[i * R:(i + 1) * R, :]
        ref = b[i * R - 1:i * R, :] if i > 0 else jnp.zeros((1, GLA_DK), F32)
        qs = q[i * R:(i + 1) * R, :] * jnp.exp(bi - ref)
        ks = k * jnp.exp(jnp.minimum(ref - b, cap))
        return lax.dot_general(qs.astype(BF16), ks.astype(BF16), nt, preferred_element_type=F32)

    @pl.when(sub_decay <= GLA_SAFE_DECAY)
    def _():
        for c in range(nchunk):
            for i in range(nsub):
                att_ref[c, i * R:(i + 1) * R, :] = jnp.where(col_r <= row_r + i * R,
                                                             scores_matmul(c, i, GLA_SAFE_DECAY), 0.0)

    @pl.when(sub_decay > GLA_SAFE_DECAY)
    def _():
        for c in range(nchunk):
            q, k, b = qs_all[c], ks_all[c], bs_all[c]
            for i in range(nsub):
                bi = b[i * R:(i + 1) * R, :]
                qi = q[i * R:(i + 1) * R, :]
                ki = k[i * R:(i + 1) * R, :]
                dg = jnp.zeros((R, C), F32)
                for s in range(R):
                    w = jnp.exp(jnp.minimum(bi - bi[s:s + 1, :], 0.0))
                    cs = jnp.sum(qi * ki[s:s + 1, :] * w, axis=-1, keepdims=True)
                    dg = jnp.where((col_r == i * R + s) & (row_r >= s), cs, dg)
                if i > 0:
                    dg = dg + jnp.where(col_r < i * R, scores_matmul(c, i, 0.0), 0.0)
                att_ref[c, i * R:(i + 1) * R, :] = dg

    for c in range(nchunk):
        rs = slice(c * C, (c + 1) * C)
        q, k, b = qs_all[c], ks_all[c], bs_all[c]
        v = v_ref[0, rs, :]
        state = st_ref[...]
        o = jnp.dot((q * jnp.exp(b)).astype(BF16), state.astype(BF16), preferred_element_type=F32)
        o = o + jnp.dot(att_ref[c].astype(BF16), v, preferred_element_type=F32)

        b_last = b[C - 1:C, :]
        kd = k * jnp.exp(b_last - b)
        dec = jnp.broadcast_to(jnp.exp(b_last), (C, GLA_DK)).T
        upd = jnp.dot(kd.T.astype(BF16), v, preferred_element_type=F32)
        st_ref[...] = state * jnp.concatenate([dec] * (GLA_DV // GLA_DK), axis=1) + upd

        on = o * lax.rsqrt(jnp.mean(o * o, axis=-1, keepdims=True) + EPS) * ng_ref[...]
        o_ref[0, rs, :] = (on * _silu(z_ref[0, rs, :].astype(F32))).astype(o_ref.dtype)


def gla_mixer(proj, gk_aux, w_gk2, b_gk2, norm_g):
    bsz, seq, _ = proj.shape
    ts = min(GLA_TS, seq)
    w2 = jnp.zeros((LANES, GLA_HEADS * GLA_DK), F32).at[:GLA_RANK].set(w_gk2.astype(F32)).astype(BF16)
    return pl.pallas_call(
        functools.partial(_gla_body, ts=ts),
        grid=(bsz, GLA_HEADS, seq // ts),
        in_specs=[
            pl.BlockSpec((1, ts, GLA_DK), lambda b, h, s: (b, s, h)),
            pl.BlockSpec((1, ts, GLA_DK), lambda b, h, s: (b, s, GLA_HEADS + h)),
            pl.BlockSpec((1, ts, GLA_DV), lambda b, h, s: (b, s, 2 + h)),
            pl.BlockSpec((1, ts, GLA_DV), lambda b, h, s: (b, s, 6 + h)),
            pl.BlockSpec((1, ts, LANES), lambda b, h, s: (b, s, 0)),
            pl.BlockSpec((LANES, GLA_DK), lambda b, h, s: (0, h)),
            pl.BlockSpec((1, GLA_DK), lambda b, h, s: (0, h)),
            pl.BlockSpec((1, GLA_DV), lambda b, h, s: (0, 0)),
        ],
        out_specs=pl.BlockSpec((1, ts, GLA_DV), lambda b, h, s: (b, s, h)),
        out_shape=jax.ShapeDtypeStruct((bsz, seq, D_INNER), BF16),
        scratch_shapes=[pltpu.VMEM((GLA_DK, GLA_DV), F32), pltpu.VMEM((ts // GLA_CHUNK, GLA_CHUNK, GLA_CHUNK), F32)],
        compiler_params=_cparams(("parallel", "parallel", "arbitrary")),
        name="gla_mixer",
    )(proj, proj, proj, proj, gk_aux, w2, b_gk2.reshape(1, -1).astype(F32), norm_g.reshape(1, -1).astype(F32))


def _norm_rope(x, g, cos, sin):
    lane = lax.broadcasted_iota(jnp.int32, (1, LANES), 1)
    first = (lane % DIFF_HALF) < (DIFF_HALF // 2)
    gr = lax.broadcasted_iota(jnp.int32, (LANES, LANES), 0) // DIFF_HALF
    gc = lax.broadcasted_iota(jnp.int32, (LANES, LANES), 1) // DIFF_HALF
    gmat = jnp.where(gr == gc, 1.0 / DIFF_HALF, 0.0).astype(BF16)
    x2 = x * x
    x2h = x2.astype(BF16)
    x2l = (x2 - x2h.astype(F32)).astype(BF16)
    ms = jnp.dot(x2h, gmat, preferred_element_type=F32) + jnp.dot(x2l, gmat, preferred_element_type=F32)
    xn = x * lax.rsqrt(ms + EPS) * g
    partner = jnp.where(first, pltpu.roll(xn, LANES - DIFF_HALF // 2, 1), pltpu.roll(xn, DIFF_HALF // 2, 1))
    return xn * cos + partner * jnp.where(first, -sin, sin)


def _diff_prep_body(x_ref, g_ref, cos_ref, sin_ref, o_ref):
    for c in range(D_INNER // LANES):
        x = x_ref[0, :, c * LANES:(c + 1) * LANES].astype(F32)
        o_ref[0, 0, :, c * LANES:(c + 1) * LANES] = _norm_rope(x, g_ref[0], cos_ref[0], sin_ref[0]).astype(o_ref.dtype)


def diff_prep(proj, gains, cos_t, sin_t):
    bsz, seq, _ = proj.shape
    tm = min(POST_TM, seq)
    return pl.pallas_call(
        _diff_prep_body,
        grid=(2, bsz, seq // tm),
        in_specs=[
            pl.BlockSpec((1, tm, D_INNER), lambda w, b, s: (b, s, w)),
            pl.BlockSpec((1, 1, LANES), lambda w, b, s: (w, 0, 0)),
            pl.BlockSpec((1, tm, LANES), lambda w, b, s: (b, s, 0)),
            pl.BlockSpec((1, tm, LANES), lambda w, b, s: (b, s, 0)),
        ],
        out_specs=pl.BlockSpec((1, 1, tm, D_INNER), lambda w, b, s: (w, b, s, 0)),
        out_shape=jax.ShapeDtypeStruct((2, bsz, seq, D_INNER), BF16),
        compiler_params=_cparams(("parallel", "parallel", "parallel")),
        name="diff_prep",
    )(proj, gains, cos_t, sin_t)


def _diff_epilogue(on, lam, sg, z, scale_out):
    r = on.shape[0] // 2
    o = on[:r] - lam * on[r:]
    o = o * lax.rsqrt(jnp.mean(o * o, axis=-1, keepdims=True) + EPS) * sg * scale_out
    return o * _silu(z.astype(F32))


def _diff_attn_online_body(q_ref, k_ref, v_ref, z_ref, lam_ref, sg_ref, o_ref, *, tq, scale_out):
    qi = pl.program_id(2)
    q = q_ref[0, 0]
    lane = lax.broadcasted_iota(jnp.int32, (tq, LANES), 1)
    zero = jnp.zeros_like(q)
    qq = jnp.concatenate([jnp.where(lane < DIFF_HALF, q, zero), jnp.where(lane >= DIFF_HALF, q, zero)], axis=0)

    def block(j, carry, masked):
        m, l, acc = carry
        k0 = pl.multiple_of(j * tq, tq)
        kb = k_ref[0, 0, pl.ds(k0, tq), :]
        vb = v_ref[0, pl.ds(k0, tq), :]
        s = lax.dot_general(qq, kb, (((1,), (1,)), ((), ())), preferred_element_type=F32)
        if masked:
            r = lax.broadcasted_iota(jnp.int32, (2 * tq, tq), 0)
            r = jnp.where(r >= tq, r - tq, r)
            cidx = lax.broadcasted_iota(jnp.int32, (2 * tq, tq), 1)
            s = jnp.where(cidx <= r, s, -jnp.inf)
        m_new = jnp.maximum(m, jnp.max(s, axis=-1, keepdims=True))
        alpha = jnp.exp2(m - m_new)
        p = jnp.exp2(s - m_new)
        l = alpha * l + jnp.sum(p, axis=-1, keepdims=True)
        acc = alpha * acc + jnp.dot(p.astype(BF16), vb, preferred_element_type=F32)
        return m_new, l, acc

    init = (jnp.full((2 * tq, 1), -jnp.inf, F32), jnp.zeros((2 * tq, 1), F32), jnp.zeros((2 * tq, LANES), F32))
    carry = lax.fori_loop(0, qi, lambda j, cr: block(j, cr, False), init)
    m, l, acc = block(qi, carry, True)
    o_ref[0] = _diff_epilogue(acc / l, lam_ref[...], sg_ref[...], z_ref[0], scale_out).astype(o_ref.dtype)


def _diff_attn_direct_body(qraw_ref, kraw_ref, v_ref, z_ref, g_ref, cos_ref, sin_ref, lam_ref, sg_ref, o_ref,
                           q_ref, k_ref, *, seq, scale_out):
    tq, hq = DIFF_TQ, DIFF_TQ // 2
    def prep(i):
        rows = pl.ds(i * tq, tq)
        cos, sin = cos_ref[0, rows, :], sin_ref[0, rows, :]
        q_ref[0, 0, rows, :] = _norm_rope(qraw_ref[0, rows, :].astype(F32), g_ref[0], cos, sin).astype(BF16)
        k_ref[0, 0, rows, :] = _norm_rope(kraw_ref[0, rows, :].astype(F32), g_ref[1], cos, sin).astype(BF16)

    lane =lax.broadcasted_iota(jnp.int32, (tq, LANES), 1)
    tri = (lax.broadcasted_iota(jnp.int32, (2 * hq, hq), 1)
           <= lax.broadcasted_iota(jnp.int32, (2 * hq, hq), 0) % hq)
    nt = (((1,), (1,)), ((), ()))

    def lanesum(p):
        acc = p[:, :LANES]
        for c in range(1, p.shape[1] // LANES):
            acc = acc + p[:, c * LANES:(c + 1) * LANES]
        return acc

    def qtile(i):
        r0 = i * tq
        q = q_ref[0, 0, pl.ds(r0, tq), :]
        zero = jnp.zeros_like(q)
        q0 = jnp.where(lane < DIFF_HALF, q, zero)
        q1 = jnp.where(lane >= DIFF_HALF, q, zero)
        qq = jnp.concatenate([q0[:hq], q1[:hq], q0[hq:], q1[hq:]], axis=0)

        def offdiag(j, carry):
            acc, l = carry
            c0 = j * tq
            p = jnp.exp2(lax.dot_general(qq, k_ref[0, 0, pl.ds(c0, tq), :], nt, preferred_element_type=F32))
            return (acc + jnp.dot(p.astype(BF16), v_ref[0, pl.ds(c0, tq), :], preferred_element_type=F32),
                    l + lanesum(p))

        carry = (jnp.zeros((2 * tq, LANES), F32), jnp.zeros((2 * tq, LANES), F32))
        for j in range(i):
            carry = offdiag(j, carry)
        acc, l = carry
        r1 = r0 + hq
        k_a, v_a = k_ref[0, 0, pl.ds(r0, hq), :], v_ref[0, pl.ds(r0, hq), :]
        k_b, v_b = k_ref[0, 0, pl.ds(r1, hq), :], v_ref[0, pl.ds(r1, hq), :]
        qq_a, qq_b = qq[:tq], qq[tq:]
        p_aa = jnp.where(tri, jnp.exp2(lax.dot_general(qq_a, k_a, nt, preferred_element_type=F32)), 0.0)
        p_ba = jnp.exp2(lax.dot_general(qq_b, k_a, nt, preferred_element_type=F32))
        p_bb = jnp.where(tri, jnp.exp2(lax.dot_general(qq_b, k_b, nt, preferred_element_type=F32)), 0.0)
        acc_a = acc[:tq] + jnp.dot(p_aa.astype(BF16), v_a, preferred_element_type=F32)
        acc_b = (acc[tq:] + jnp.dot(p_ba.astype(BF16), v_a, preferred_element_type=F32)
                 + jnp.dot(p_bb.astype(BF16), v_b, preferred_element_type=F32))
        l_a = jnp.sum(l[:tq] + lanesum(p_aa), axis=-1, keepdims=True)
        l_b = jnp.sum(l[tq:] + lanesum(p_ba) + lanesum(p_bb), axis=-1, keepdims=True)
        o_ref[0, pl.ds(r0, hq), :] = _diff_epilogue(
            acc_a / l_a, lam_ref[...], sg_ref[...], z_ref[0, pl.ds(r0, hq), :], scale_out).astype(o_ref.dtype)
        o_ref[0, pl.ds(r1, hq), :] = _diff_epilogue(
            acc_b / l_b, lam_ref[...], sg_ref[...], z_ref[0, pl.ds(r1, hq), :], scale_out).astype(o_ref.dtype)

    for i in range(seq // tq):
        prep(i)
    for i in range(seq // tq):
        qtile(i)


def diff_attn(proj, gains, cos_t, sin_t, lam_vec, subln_g, lam_init, score_bound_log2):
    bsz, seq, _ = proj.shape
    koff = D_INNER // LANES
    voff = 2 * D_INNER // LANES
    zoff = 3 * D_INNER // LANES
    sg = subln_g.reshape(1, LANES).astype(F32)
    vec = pl.BlockSpec((1, LANES), lambda *_: (0, 0))

    def online(proj, gains, cos_t, sin_t, lam_vec, sg):
        qk = diff_prep(proj, gains, cos_t, sin_t)
        tq = min(DIFF_TQ, seq)
        return pl.pallas_call(
            functools.partial(_diff_attn_online_body, tq=tq, scale_out=1.0 - lam_init),
            grid=(bsz, DIFF_HEADS, seq // tq),
            in_specs=[
                pl.BlockSpec((1, 1, tq, LANES), lambda b, h, i: (0, b, i, h)),
                pl.BlockSpec((1, 1, seq, LANES), lambda b, h, i: (1, b, 0, h)),
                pl.BlockSpec((1, seq, LANES), lambda b, h, i: (b, 0, voff + h)),
                pl.BlockSpec((1, tq, LANES), lambda b, h, i: (b, i, zoff + h)),
                vec, vec,
            ],
            out_specs=pl.BlockSpec((1, tq, LANES), lambda b, h, i: (b, i, h)),
            out_shape=jax.ShapeDtypeStruct((bsz, seq, D_INNER), BF16),
            compiler_params=_cparams(("parallel", "parallel", "arbitrary")),
            name="diff_attn_online",
        )(qk, qk, proj, proj, lam_vec, sg)

    def direct(proj, gains, cos_t, sin_t, lam_vec, sg):
        return pl.pallas_call(
            functools.partial(_diff_attn_direct_body, seq=seq, scale_out=1.0 - lam_init),
            grid=(bsz, DIFF_HEADS),
            in_specs=[
                pl.BlockSpec((1, seq, LANES), lambda b, h: (b, 0, h)),
                pl.BlockSpec((1, seq, LANES), lambda b, h: (b, 0, koff + h)),
                pl.BlockSpec((1, seq, LANES), lambda b, h: (b, 0, voff + h)),
                pl.BlockSpec((1, seq, LANES), lambda b, h: (b, 0, zoff + h)),
                pl.BlockSpec((2, 1, LANES), lambda b, h: (0, 0, 0)),
                pl.BlockSpec((1, seq, LANES), lambda b, h: (b, 0, 0)),
                pl.BlockSpec((1, seq, LANES), lambda b, h: (b, 0, 0)),
                vec, vec,
            ],
            out_specs=pl.BlockSpec((1, seq, LANES), lambda b, h: (b, 0, h)),
            out_shape=jax.ShapeDtypeStruct((bsz, seq, D_INNER), BF16),
            scratch_shapes=[pltpu.VMEM((1, 1, seq, LANES), BF16), pltpu.VMEM((1, 1, seq, LANES), BF16)],
            compiler_params=_cparams(("parallel", "parallel")),
            name="diff_attn_direct",
        )(proj, proj, proj, proj, gains, cos_t, sin_t, lam_vec, sg)

    operands = (proj, gains, cos_t, sin_t, lam_vec, sg)
    if seq % DIFF_TQ != 0:
        return online(*operands)
    return lax.cond(score_bound_log2 <= DIFF_DIRECT_MAX_LOG2, direct, online, *operands)


def _ssd_body(xbc_ref, z_ref, dt_ref, cw_ref, cb_ref, dtb_ref, alog_ref, dsk_ref, ng_ref, e64_ref, e512_ref,
              o_ref, xs_ref, st_ref, *, ts):
    L = SSD_CHUNK
    gw = SSD_HPG * SSD_HEAD_DIM
    pad = 8

    @pl.when(pl.program_id(1) == 0)
    def _():
        st_ref[...] = jnp.zeros_like(st_ref)
        xs_ref[0:pad, :] = jnp.zeros((pad, SSD_CONV_CH), F32)

    @pl.when(pl.program_id(1) > 0)
    def _():
        xs_ref[0:pad, :] = xs_ref[ts:ts + pad, :]

    xs_ref[pad:pad + ts, :] = xbc_ref[0].astype(F32)
    cw = cw_ref[...]
    xt = xs_ref[0:pad + ts, :]
    u = xt * cw[0:1, :]
    for w in range(1, SSD_CONV):
        u = pltpu.roll(u, 1, 0) + xt * cw[w:w + 1, :]
    xc_all = _silu(u[pad:, :] + cb_ref[...])

    tri3 = _tri3(L)
    lane_grp = lax.broadcasted_iota(jnp.int32, (L, LANES), 1) // SSD_HEADS
    row4 = lax.broadcasted_iota(jnp.int32, (L, SSD_HPG * L), 0)
    col4 = lax.broadcasted_iota(jnp.int32, (L, SSD_HPG * L), 1) % L
    causal4 = col4 <= row4
    br = lax.broadcasted_iota(jnp.int32, (SSD_HPG * L, gw), 0) // L
    bc = lax.broadcasted_iota(jnp.int32, (SSD_HPG * L, gw), 1) // SSD_HEAD_DIM
    blockdiag = br == bc
    a_neg = -jnp.exp(alog_ref[...]) * math.log2(math.e)

    for c in range(ts // L):
        rs = slice(c * L, (c + 1) * L)
        xc = xc_all[rs, :]
        xr = dt_ref[0, rs, :] + dtb_ref[...]
        dt = jnp.maximum(xr, 0.0) + jnp.log(1.0 + jnp.exp(-jnp.abs(xr)))
        cum = _cumsum_rows(tri3, dt * a_neg)
        cum_p = _pack_split3(cum, lane_grp)
        cum64 = jnp.dot(cum_p, e64_ref[...], preferred_element_type=F32)
        dt64 = jnp.dot(_pack_split3(dt, lane_grp), e64_ref[...], preferred_element_type=F32)
        cum512 = jnp.dot(cum_p, e512_ref[...], preferred_element_type=F32)
        cum_t = cum.T
        last64 = cum64[L - 1:L, :]
        ys = []
        for g in range(SSD_GROUPS):
            xg = xc[:, g * gw:(g + 1) * gw]
            bg = xc[:, D_INNER + g * SSD_STATE:D_INNER + (g + 1) * SSD_STATE]
            cg = xc[:, D_INNER + SSD_GROUPS * SSD_STATE + g * SSD_STATE:
                    D_INNER + SSD_GROUPS * SSD_STATE + (g + 1) * SSD_STATE]
            cgb = cg.astype(BF16)
            cb = lax.dot_general(cgb, bg.astype(BF16), (((1,), (1,)), ((), ())), preferred_element_type=F32)
            ce = cum512[:, g * SSD_HPG * L:(g + 1) * SSD_HPG * L]
            crow = jnp.concatenate([cum_t[g * SSD_HPG + hh:g * SSD_HPG + hh + 1, :] for hh in range(SSD_HPG)],
                                   axis=1)
            m = jnp.where(causal4, jnp.exp2(jnp.minimum(ce - crow, 0.0)), 0.0) * jnp.concatenate([cb] * SSD_HPG, axis=1)
            cg64 = cum64[:, g * gw:(g + 1) * gw]
            dg64 = dt64[:, g * gw:(g + 1) * gw]
            xdt = xg * dg64
            rhs = jnp.where(blockdiag, jnp.concatenate([xdt] * SSD_HPG, axis=0), 0.0)
            y = jnp.dot(m.astype(BF16), rhs.astype(BF16), preferred_element_type=F32)
            state = st_ref[g]
            y = y + jnp.dot(cgb, state.astype(BF16), preferred_element_type=F32) * jnp.exp2(cg64)
            lg = last64[:, g * gw:(g + 1) * gw]
            xw = xg * (jnp.exp2(lg - cg64) * dg64)
            st_ref[g] = state * jnp.exp2(lg) + jnp.dot(bg.T.astype(BF16), xw.astype(BF16), preferred_element_type=F32)
            ys.append(y + dsk_ref[:, g * gw:(g + 1) * gw] * xg)
        y = jnp.concatenate(ys, axis=1) * _silu(z_ref[0, rs, :].astype(F32))
        y = y * lax.rsqrt(jnp.mean(y * y, axis=-1, keepdims=True) + EPS) * ng_ref[...]
        o_ref[0, rs, :] = y.astype(o_ref.dtype)


def _ssd_expanders():
    e64 = np.zeros((LANES, D_INNER), np.float32)
    e512 = np.zeros((LANES, SSD_HEADS * SSD_CHUNK), np.float32)
    for rep in range(SSD_REPS):
        for h in range(SSD_HEADS):
            e64[rep * SSD_HEADS + h, h * SSD_HEAD_DIM:(h + 1) * SSD_HEAD_DIM] = 1.0
            e512[rep * SSD_HEADS + h, h * SSD_CHUNK:(h + 1) * SSD_CHUNK] = 1.0
    return jnp.asarray(e64, BF16), jnp.asarray(e512, BF16)


def ssd_mixer(proj, dt_aux, conv_w, conv_b, dt_bias, a_log, d_skip, norm_g):
    bsz, seq, _ = proj.shape
    ts = min(SSD_TS, seq)
    e64, e512 = _ssd_expanders()

    def pad128(v):
        return jnp.zeros((1, LANES), F32).at[0, :SSD_REPS * SSD_HEADS].set(jnp.tile(v.astype(F32), SSD_REPS))

    dsk = jnp.repeat(d_skip.astype(F32), SSD_HEAD_DIM).reshape(1, D_INNER)
    full = lambda shape: pl.BlockSpec(shape, lambda b, s: (0,) * len(shape))
    return pl.pallas_call(
        functools.partial(_ssd_body, ts=ts),
        grid=(bsz, seq // ts),
        in_specs=[
            pl.BlockSpec((1, ts, SSD_CONV_CH), lambda b, s: (b, s, 0)),
            pl.BlockSpec((1, ts, D_INNER), lambda b, s: (b, s, 2)),
            pl.BlockSpec((1, ts, LANES), lambda b, s: (b, s, 0)),
            full((SSD_CONV, SSD_CONV_CH)),
            full((1, SSD_CONV_CH)),
            full((1, LANES)),
            full((1, LANES)),
            full((1, D_INNER)),
            full((1, D_INNER)),
            full((LANES, D_INNER)),
            full((LANES, SSD_HEADS * SSD_CHUNK)),
        ],
        out_specs=pl.BlockSpec((1, ts, D_INNER), lambda b, s: (b, s, 0)),
        out_shape=jax.ShapeDtypeStruct((bsz, seq, D_INNER), BF16),
        scratch_shapes=[pltpu.VMEM((ts + 16, SSD_CONV_CH), F32),
                        pltpu.VMEM((SSD_GROUPS, SSD_STATE, SSD_HPG * SSD_HEAD_DIM), F32)],
        compiler_params=_cparams(("parallel", "arbitrary")),
        name="ssd_mixer",
    )(proj, proj, dt_aux, conv_w.astype(F32), conv_b.reshape(1, -1).astype(F32), pad128(dt_bias), pad128(a_log),
      dsk, norm_g.reshape(1, -1).astype(F32), e64, e512)


def _pad_cols(w, width=LANES):
    return jnp.zeros((w.shape[0], width), F32).at[:, :w.shape[1]].set(w.astype(F32))


def s5_layer(x, norm_g, w_in, lam_re, lam_im, log_step, b_re, b_im, c_re, c_im, d_skip, w_glu):
    bsz, seq, d = x.shape
    proj = in_proj(x.reshape(bsz * seq, d), norm_g, w_in.astype(BF16)).reshape(bsz, seq, -1)
    y = s5_slab(proj, *s5_slab_params(lam_re, lam_im, log_step, b_re, b_im, c_re, c_im))
    branch = s5_post(y, proj, d_skip, w_glu.astype(BF16))
    return branch, proj, (2 * D_INNER) // D_X, (2 * D_INNER) // D_X + 1


def gla_layer(x, norm_g, w_in, w_gk2, b_gk2, gla_norm_g):
    bsz, seq, d = x.shape
    nqk = GLA_HEADS * GLA_DK
    c0 = 2 * nqk + D_INNER
    w_main = jnp.concatenate([w_in[:, :c0], w_in[:, c0 + GLA_RANK:]], axis=1).astype(BF16)
    proj, gk_aux = in_proj(x.reshape(bsz * seq, d), norm_g, w_main, _pad_cols(w_in[:, c0:c0 + GLA_RANK]))
    proj = proj.reshape(bsz, seq, -1)
    branch = gla_mixer(proj, gk_aux.reshape(bsz, seq, LANES), w_gk2, b_gk2, gla_norm_g)
    zx_col = (c0 + D_INNER) // D_X
    return branch, proj, zx_col, zx_col + 1


def diff_layer(x, norm_g, w_in, cos_t, sin_t, q_g, k_g, lq1, lk1, lq2, lk2, subln_g, lam_init):
    bsz, seq, d = x.shape
    proj = in_proj(x.reshape(bsz * seq, d), norm_g, w_in.astype(BF16)).reshape(bsz, seq, -1)
    q_scale = (DIFF_HALF ** -0.5) * math.log2(math.e)
    gains = jnp.stack([jnp.tile(q_g.astype(F32), LANES // DIFF_HALF) * q_scale,
                       jnp.tile(k_g.astype(F32), LANES // DIFF_HALF)]).reshape(2, 1, LANES)
    score_bound_log2 = (BF16_ROUNDING_SLACK * DIFF_HALF * q_scale
                        * jnp.max(jnp.abs(q_g.astype(F32))) * jnp.max(jnp.abs(k_g.astype(F32))))
    lam = (jnp.exp(jnp.sum(lq1.astype(F32) * lk1.astype(F32)))
           - jnp.exp(jnp.sum(lq2.astype(F32) * lk2.astype(F32))) + lam_init)
    lam_vec = jnp.full((1, LANES), lam, F32)
    branch = diff_attn(proj, gains, cos_t, sin_t, lam_vec, subln_g, lam_init, score_bound_log2)
    zx_col = (4 * D_INNER) // D_X
    return branch, proj, zx_col, zx_col + 1


def ssd_layer(x, norm_g, w_in, conv_w, conv_b, dt_bias, a_log, d_skip, ssd_norm_g):
    bsz, seq, d = x.shape
    c0 = SSD_CONV_CH
    w_main = jnp.concatenate([w_in[:, :c0], w_in[:, c0 + SSD_HEADS:]], axis=1).astype(BF16)
    w_dt = jnp.tile(w_in[:, c0:c0 + SSD_HEADS], (1, SSD_REPS))
    proj, dt_aux = in_proj(x.reshape(bsz * seq, d), norm_g, w_main, _pad_cols(w_dt))
    proj = proj.reshape(bsz, seq, -1)
    branch = ssd_mixer(proj, dt_aux.reshape(bsz, seq, LANES), conv_w, conv_b, dt_bias, a_log, d_skip, ssd_norm_g)
    zx_col = (c0 + D_INNER) // D_X
    return branch, proj, zx_col, zx_col + 1


def rope_tables(positions):
    inv = ROPE_THETA ** (-jnp.arange(0, DIFF_HALF, 2, dtype=F32) / DIFF_HALF)
    ang = positions.astype(F32)[..., None] * inv
    reps = LANES // (DIFF_HALF // 2)
    return jnp.tile(jnp.cos(ang), (1, 1, reps)), jnp.tile(jnp.sin(ang), (1, 1, reps))


def kernel(x, mem, positions, norm_g, w_out, mem_norm_g, w_mem_kv, xq_g, xk_g, s5_w_in, s5_lam_re, s5_lam_im, s5_log_step, s5_b_re, s5_b_im, s5_c_re, s5_c_im, s5_d, s5_w_glu, gla_w_in, gla_w_gk2, gla_b_gk2, gla_norm_g, diff_w_in, diff_q_g, diff_k_g, diff_lq1, diff_lk1, diff_lq2, diff_lk2, diff_subln_g, ssd_w_in, ssd_conv_w, ssd_conv_b, ssd_dt_bias, ssd_a_log, ssd_d, ssd_norm_g):
    depth = norm_g.shape[0]
    k_all, v_all = mem_kv(mem, mem_norm_g, w_mem_kv, xk_g)
    cos_t, sin_t = rope_tables(positions)
    w_out_b = w_out.astype(BF16)
    for i in range(depth):
        kind, j = i % 4, i // 4
        if kind == 0:
            branch, proj, zc, qc = s5_layer(x, norm_g[i], s5_w_in[j], s5_lam_re[j], s5_lam_im[j], s5_log_step[j],
                                            s5_b_re[j], s5_b_im[j], s5_c_re[j], s5_c_im[j], s5_d[j], s5_w_glu[j])
        elif kind == 1:
            branch, proj, zc, qc = gla_layer(x, norm_g[i], gla_w_in[j], gla_w_gk2[j], gla_b_gk2[j], gla_norm_g[j])
        elif kind == 2:
            lam_init = 0.8 - 0.6 * math.exp(-0.3 * i)
            branch, proj, zc, qc = diff_layer(x, norm_g[i], diff_w_in[j], cos_t, sin_t, diff_q_g[j], diff_k_g[j],
                                              diff_lq1[j], diff_lk1[j], diff_lq2[j], diff_lk2[j], diff_subln_g[j],
                                              lam_init)
        else:
            branch, proj, zc, qc = ssd_layer(x, norm_g[i], ssd_w_in[j], ssd_conv_w[j], ssd_conv_b[j], ssd_dt_bias[j],
                                             ssd_a_log[j], ssd_d[j], ssd_norm_g[j])
        x = out_stage(branch, proj, zc, qc, k_all[i], v_all[i], x, w_out_b[i], xq_g[i])
    return x
```

```python
import functools
import math

import numpy as np
import jax
import jax.numpy as jnp
from jax import lax
from jax.experimental import pallas as pl
from jax.experimental.pallas import tpu as pltpu

F32 = jnp.float32
BF16 = jnp.bfloat16

D_MODEL = 1024
D_INNER = 2048
EPS = 1e-6
MEM_LEN = 256
X_HEADS = 4
X_HEAD_DIM = 128
D_X = X_HEADS * X_HEAD_DIM
D_GATE = D_INNER + D_X
S5_GROUP = 16
S5_GROUPS = D_INNER // S5_GROUP
S5_STATE = 64
GLA_HEADS = 4
GLA_DK = 128
GLA_DV = 512
GLA_RANK = 16
GLA_TAU = 16.0
DIFF_HEADS = 16
DIFF_HALF = 64
DIFF_VDIM = 128
ROPE_THETA = 10000.0
SSD_HEAD_DIM = 64
SSD_HEADS = 32
SSD_GROUPS = 8
SSD_HPG = 4
SSD_STATE = 128
SSD_CONV = 4
SSD_CONV_CH = D_INNER + 2 * SSD_GROUPS * SSD_STATE

LANES = 128
MXU_WIDTH = 256
VMEM_LIMIT = 48 * 1024 * 1024

PROJ_TM = 1024
PROJ_TN_MAX = 2560
OUT_TM = 1024
POST_TM = 512
S5_BT = 16
S5_T = 16
S5_TS = 512
GLA_TS = 1024
GLA_CHUNK = 128
GLA_SUB = 16
GLA_SAFE_DECAY = 40.0
DIFF_TQ = 512
DIFF_DIRECT_MAX_LOG2 = 64.0
BF16_ROUNDING_SLACK = 1.02
SSD_TS = 512
SSD_CHUNK = 128
SSD_REPS = 3


def _cparams(sem):
    return pltpu.CompilerParams(dimension_semantics=sem, vmem_limit_bytes=VMEM_LIMIT)


def _sigmoid(x):
    return 1.0 / (1.0 + jnp.exp(-x))


def _silu(x):
    return x * _sigmoid(x)


def _split3(x):
    x1 = x.astype(BF16)
    r1 = x - x1.astype(F32)
    x2 = r1.astype(BF16)
    x3 = (r1 - x2.astype(F32)).astype(BF16)
    return x1, x2, x3


def _cumsum_rows(tri3, x):
    return jnp.dot(tri3, jnp.concatenate(_split3(x), axis=0), preferred_element_type=F32)


def _pack_split3(x, lane_grp):
    x1, x2, x3 = _split3(x)
    zero = jnp.zeros_like(x1)
    return jnp.where(lane_grp == 0, x1, jnp.where(lane_grp == 1, x2, jnp.where(lane_grp == 2, x3, zero)))


def _tri3(n):
    row = lax.broadcasted_iota(jnp.int32, (n, n), 0)
    col = lax.broadcasted_iota(jnp.int32, (n, n), 1)
    tri = (col <= row).astype(BF16)
    return jnp.concatenate([tri, tri, tri], axis=1)


def _in_proj_body(x_ref, g_ref, w_ref, *rest, has_aux):
    if has_aux:
        waux_ref, o_ref, oaux_ref, h_ref = rest
    else:
        o_ref, h_ref = rest

    @pl.when(pl.program_id(1) == 0)
    def _():
        x = x_ref[...]
        h = x * lax.rsqrt(jnp.mean(x * x, axis=-1, keepdims=True) + EPS) * g_ref[...]
        hb = h.astype(BF16)
        h_ref[...] = hb
        if has_aux:
            oaux_ref[...] = jnp.dot(hb, waux_ref[...], preferred_element_type=F32)

    o_ref[...] = jnp.dot(h_ref[...], w_ref[...], preferred_element_type=F32).astype(o_ref.dtype)


def in_proj(x2d, g, w_bf16, w_aux=None):
    n, d = x2d.shape
    nc = w_bf16.shape[1]
    tm = min(PROJ_TM, n)
    tn = max(t for t in range(MXU_WIDTH, PROJ_TN_MAX + 1, MXU_WIDTH) if nc % t == 0)
    assert n % tm == 0
    has_aux = w_aux is not None
    in_specs = [
        pl.BlockSpec((tm, d), lambda i, j: (i, 0)),
        pl.BlockSpec((1, d), lambda i, j: (0, 0)),
        pl.BlockSpec((d, tn), lambda i, j: (0, j)),
    ]
    args = [x2d, g.reshape(1, d).astype(F32), w_bf16]
    out_shape = [jax.ShapeDtypeStruct((n, nc), BF16)]
    out_specs = [pl.BlockSpec((tm, tn), lambda i, j: (i, j))]
    if has_aux:
        in_specs.append(pl.BlockSpec((d, LANES), lambda i, j: (0, 0)))
        args.append(w_aux.astype(BF16))
        out_shape.append(jax.ShapeDtypeStruct((n, LANES), F32))
        out_specs.append(pl.BlockSpec((tm, LANES), lambda i, j: (i, 0)))
    res = pl.pallas_call(
        functools.partial(_in_proj_body, has_aux=has_aux),
        grid=(n // tm, nc // tn),
        in_specs=in_specs,
        out_specs=out_specs,
        out_shape=out_shape,
        scratch_shapes=[pltpu.VMEM((tm, d), BF16)],
        compiler_params=_cparams(("parallel", "arbitrary")),
        name="in_proj_aux" if has_aux else "in_proj",
    )(*args)
    return (res[0], res[1]) if has_aux else res[0]


def _mem_kv_body(mem_ref, mg_ref, w_ref, kg_ref, k_ref, v_ref):
    m = mem_ref[0]
    mn = m * lax.rsqrt(jnp.mean(m * m, axis=-1, keepdims=True) + EPS) * mg_ref[...]
    kv = jnp.dot(mn.astype(BF16), w_ref[0], preferred_element_type=F32)
    kg = kg_ref[0]
    for h in range(X_HEADS):
        kh = kv[:, h * X_HEAD_DIM:(h + 1) * X_HEAD_DIM]
        kh = kh * lax.rsqrt(jnp.mean(kh * kh, axis=-1, keepdims=True) + EPS) * kg
        k_ref[0, 0, :, h * X_HEAD_DIM:(h + 1) * X_HEAD_DIM] = kh.astype(BF16)
    v_ref[0, 0] = kv[:, D_X:].astype(BF16)


def mem_kv(mem, mem_norm_g, w_mem_kv, xk_g):
    bsz, mlen, d = mem.shape
    depth = w_mem_kv.shape[0]
    return pl.pallas_call(
        _mem_kv_body,
        grid=(depth, bsz),
        in_specs=[
            pl.BlockSpec((1, mlen, d), lambda l, b: (b, 0, 0)),
            pl.BlockSpec((1, d), lambda l, b: (0, 0)),
            pl.BlockSpec((1, d, 2 * D_X), lambda l, b: (l, 0, 0)),
            pl.BlockSpec((1, 1, X_HEAD_DIM), lambda l, b: (l, 0, 0)),
        ],
        out_specs=[
            pl.BlockSpec((1, 1, mlen, D_X), lambda l, b: (l, b, 0, 0)),
            pl.BlockSpec((1, 1, mlen, D_X), lambda l, b: (l, b, 0, 0)),
        ],
        out_shape=[jax.ShapeDtypeStruct((depth, bsz, mlen, D_X), BF16)] * 2,
        compiler_params=_cparams(("parallel", "parallel")),
        name="mem_kv",
    )(mem, mem_norm_g.reshape(1, d).astype(F32), w_mem_kv.astype(BF16),
      xk_g.reshape(depth, 1, X_HEAD_DIM).astype(F32))


def _out_body(br_ref, zx_ref, qx_ref, k_ref, v_ref, x_ref, w_ref, qg_ref, o_ref):
    q = qx_ref[0].astype(F32)
    zx = zx_ref[0].astype(F32)
    qg = qg_ref[...]
    outs = []
    for h in range(X_HEADS):
        sl = slice(h * X_HEAD_DIM, (h + 1) * X_HEAD_DIM)
        qh = q[:, sl]
        qn = qh * lax.rsqrt(jnp.mean(qh * qh, axis=-1, keepdims=True) + EPS) * qg
        s = lax.dot_general(qn.astype(BF16), k_ref[0, :, sl], (((1,), (1,)), ((), ())),
                            preferred_element_type=F32) * (X_HEAD_DIM ** -0.5)
        p = jnp.exp(s - jnp.max(s, axis=-1, keepdims=True))
        l = jnp.sum(p, axis=-1, keepdims=True)
        oh = jnp.dot(p.astype(BF16), v_ref[0, :, sl], preferred_element_type=F32) / l
        outs.append((oh * _silu(zx[:, sl])).astype(BF16))
    mem_out = jnp.concatenate(outs, axis=-1)
    acc = jnp.dot(br_ref[0], w_ref[:D_INNER, :], preferred_element_type=F32)
    acc = acc + jnp.dot(mem_out, w_ref[D_INNER:, :], preferred_element_type=F32)
    o_ref[0] = x_ref[0] + acc


def out_stage(branch, proj, zx_col, qx_col, k_l, v_l, x, w_out_bf16, xq_g):
    bsz, seq, _ = branch.shape
    tm = min(OUT_TM, seq)
    return pl.pallas_call(
        _out_body,
        grid=(bsz, seq // tm),
        in_specs=[
            pl.BlockSpec((1, tm, D_INNER), lambda b, s: (b, s, 0)),
            pl.BlockSpec((1, tm, D_X), lambda b, s: (b, s, zx_col)),
            pl.BlockSpec((1, tm, D_X), lambda b, s: (b, s, qx_col)),
            pl.BlockSpec((1, MEM_LEN, D_X), lambda b, s: (b, 0, 0)),
            pl.BlockSpec((1, MEM_LEN, D_X), lambda b, s: (b, 0, 0)),
            pl.BlockSpec((1, tm, D_MODEL), lambda b, s: (b, s, 0)),
            pl.BlockSpec((D_GATE, D_MODEL), lambda b, s: (0, 0)),
            pl.BlockSpec((1, X_HEAD_DIM), lambda b, s: (0, 0)),
        ],
        out_specs=pl.BlockSpec((1, tm, D_MODEL), lambda b, s: (b, s, 0)),
        out_shape=jax.ShapeDtypeStruct((bsz, seq, D_MODEL), F32),
        compiler_params=_cparams(("parallel", "parallel")),
        name="out_stage",
    )(branch, proj, proj, k_l, v_l, x, w_out_bf16, xq_g.reshape(1, X_HEAD_DIM).astype(F32))


def _s5_slab_body(u_ref, bd_ref, mc_ref, nc_ref, a_ref, y_ref, w_ref, m_ref, n_ref, xs_ref, sp_ref, st_ref, *, nc, bt):
    T = S5_T
    gps = LANES // S5_GROUP
    half = S5_STATE * gps

    @pl.when((pl.program_id(1) == 0) & (pl.program_id(2) == 0))
    def _():
        for s in range(T):
            for t in range(max(s - 1, 0), T):
                blk = bd_ref[0, t - s] if t >= s else jnp.zeros((LANES, LANES), BF16)
                w_ref[s * LANES:(s + 1) * LANES, t * LANES:(t + 1) * LANES] = blk
        mrow = lax.broadcasted_iota(jnp.int32, (T * LANES, half), 0) // S5_GROUP % gps
        mcol = lax.broadcasted_iota(jnp.int32, (T * LANES, half), 1) // S5_STATE
        nrow = lax.broadcasted_iota(jnp.int32, (half, T * LANES), 0) // S5_STATE
        ncol = lax.broadcasted_iota(jnp.int32, (half, T * LANES), 1) // S5_GROUP % gps
        for part in range(2):
            mt = jnp.concatenate([mc_ref[0, part]] * (half // LANES), axis=1)
            m_ref[:, part * half:(part + 1) * half] = jnp.where(mrow == mcol, mt, jnp.zeros_like(mt))
            nt = jnp.concatenate([nc_ref[0, part]] * gps, axis=0)
            n_ref[part * half:(part + 1) * half, :] = jnp.where(nrow == ncol, nt, jnp.zeros_like(nt))

    @pl.when(pl.program_id(2) == 0)
    def _():
        st_ref[...] = jnp.zeros_like(st_ref)

    v4 = pltpu.einshape("bcsl->csbl", u_ref[...].reshape(bt, nc, T, LANES))
    v = jnp.concatenate([v4[:, s].reshape(nc * bt, LANES) for s in range(T)], axis=1)
    xs_ref[...] = jnp.dot(v, m_ref[...], preferred_element_type=F32)
    a = a_ref[0]
    a_re, a_im = a[:, :half], a[:, half:]

    def step(c, carry):
        sr, si = carry
        r0 = pl.multiple_of(c * bt, bt)
        sp_ref[pl.ds(r0, bt), :half] = sr
        sp_ref[pl.ds(r0, bt), half:] = si
        x = xs_ref[pl.ds(r0, bt), :]
        return a_re * sr - a_im * si + x[:, :half], a_re * si + a_im * sr + x[:, half:]

    st = st_ref[...]
    sr, si = lax.fori_loop(0, nc, step, (st[:, :half], st[:, half:]), unroll=4)
    st_ref[:, :half] = sr
    st_ref[:, half:] = si
    y_state = jnp.dot(sp_ref[...].astype(BF16), n_ref[...], preferred_element_type=F32)
    cols = []
    for tp in range(T // 2):
        kk = (2 * tp + 2) * LANES
        cols.append(jnp.dot(v[:, :kk], w_ref[:kk, tp * 2 * LANES:(tp + 1) * 2 * LANES], preferred_element_type=F32))
    y = (y_state + jnp.concatenate(cols, axis=1)).astype(BF16)
    y4 = jnp.stack([y[:, t * LANES:(t + 1) * LANES].reshape(nc, bt, LANES) for t in range(T)], axis=1)
    y_ref[...] = pltpu.einshape("ctbl->bctl", y4).reshape(bt, nc * T, LANES)


def s5_slab(proj, bd, m_compact, n_compact, a_slab):
    bsz, seq, _ = proj.shape
    bt = S5_BT
    ts = min(S5_TS, seq)
    nslab = D_INNER // LANES
    nstate = 2 * S5_STATE * (LANES // S5_GROUP)
    assert bsz % bt == 0 and seq % ts == 0
    return pl.pallas_call(
        functools.partial(_s5_slab_body, nc=ts // S5_T, bt=bt),
        grid=(nslab, bsz // bt, seq // ts),
        in_specs=[
            pl.BlockSpec((bt, ts, LANES), lambda j, b, s: (b, s, j)),
            pl.BlockSpec((1, S5_T, LANES, LANES), lambda j, b, s: (j, 0, 0, 0)),
            pl.BlockSpec((1, 2, S5_T * LANES, LANES), lambda j, b, s: (j, 0, 0, 0)),
            pl.BlockSpec((1, 2, S5_STATE, S5_T * LANES), lambda j, b, s: (j, 0, 0, 0)),
            pl.BlockSpec((1, bt, nstate), lambda j, b, s: (j, 0, 0)),
        ],
        out_specs=pl.BlockSpec((bt, ts, LANES), lambda j, b, s: (b, s, j)),
        out_shape=jax.ShapeDtypeStruct((bsz, seq, D_INNER), BF16),
        scratch_shapes=[pltpu.VMEM((S5_T * LANES, S5_T * LANES), BF16),
                        pltpu.VMEM((S5_T * LANES, nstate), BF16),
                        pltpu.VMEM((nstate, S5_T * LANES), BF16),
                        pltpu.VMEM((ts // S5_T * bt, nstate), F32),
                        pltpu.VMEM((ts // S5_T * bt, nstate), F32),
                        pltpu.VMEM((bt, nstate), F32)],
        compiler_params=_cparams(("arbitrary", "arbitrary", "arbitrary")),
        name="s5_slab",
    )(proj, bd, m_compact, n_compact, a_slab)


def s5_slab_params(lam_re, lam_im, log_step, b_re, b_im, c_re, c_im):
    T, G, P, H = S5_T, S5_GROUPS, S5_STATE, S5_GROUP
    gps = LANES // H
    nslab = G // gps
    step = jnp.exp(log_step.astype(F32))[:, None]
    lam_re = lam_re.astype(F32)
    lam_im = lam_im.astype(F32)
    mag = jnp.exp(lam_re * step)
    lb_re = mag * jnp.cos(lam_im * step)
    lb_im = mag * jnp.sin(lam_im * step)
    den = lam_re * lam_re + lam_im * lam_im
    nr = lb_re - 1.0
    co_re = (nr * lam_re + lb_im * lam_im) / den
    co_im = (lb_im * lam_re - nr * lam_im) / den
    b_re = b_re.astype(F32)
    b_im = b_im.astype(F32)
    bb_re = co_re[..., None] * b_re - co_im[..., None] * b_im
    bb_im = co_re[..., None] * b_im + co_im[..., None] * b_re
    c_re = c_re.astype(F32)
    c_im = c_im.astype(F32)
    j = jnp.arange(T + 1, dtype=F32)[:, None, None]
    pmag = jnp.exp(j * (lam_re * step))
    pw_re = pmag * jnp.cos(j * (lam_im * step))
    pw_im = pmag * jnp.sin(j * (lam_im * step))
    cl_re = c_re[None] * pw_re[:, :, None, :] - c_im[None] * pw_im[:, :, None, :]
    cl_im = c_re[None] * pw_im[:, :, None, :] + c_im[None] * pw_re[:, :, None, :]
    kj = jnp.einsum('gjkp,gph->jgkh', jnp.concatenate([cl_re, -cl_im], axis=-1).transpose(1, 0, 2, 3),
                    jnp.concatenate([bb_re, bb_im], axis=1))
    eye = jnp.eye(gps, dtype=F32)
    bd = jnp.einsum('djgkh,gq->jdghqk', kj[:T].reshape(T, nslab, gps, H, H), eye).reshape(nslab, T, LANES, LANES)
    rev = T - 1 - jnp.arange(T)
    m_re = pw_re[rev][..., None] * bb_re[None] - pw_im[rev][..., None] * bb_im[None]
    m_im = pw_re[rev][..., None] * bb_im[None] + pw_im[rev][..., None] * bb_re[None]

    def m_rows(m):
        r = m.reshape(T, nslab, gps, P, H).transpose(1, 0, 2, 4, 3).reshape(nslab, T * LANES, P)
        return jnp.concatenate([r, r], axis=-1)

    m_compact = jnp.stack([m_rows(m_re), m_rows(m_im)], axis=1)

    def n_rows(cl):
        return cl[1:].reshape(T, nslab, gps, H, P).transpose(1, 4, 0, 2, 3).reshape(nslab, P, T * LANES)

    n_compact = jnp.stack([n_rows(cl_re), -n_rows(cl_im)], axis=1)
    a = jnp.concatenate([pw_re[T].reshape(nslab, gps * P), pw_im[T].reshape(nslab, gps * P)], axis=-1)
    a_slab = jnp.broadcast_to(a[:, None, :], (nslab, S5_BT, 2 * gps * P))
    return bd.astype(BF16), m_compact.astype(BF16), n_compact.astype(BF16), a_slab


def _s5_post_body(y_ref, u_ref, z_ref, d_ref, w_ref, o_ref):
    yv = y_ref[0].astype(F32) + d_ref[...] * u_ref[0].astype(F32)
    g = 0.5 * yv * (1.0 + jnp.tanh(math.sqrt(2.0 / math.pi) * (yv + 0.044715 * (yv * yv * yv))))
    gate = _sigmoid(jnp.dot(g.astype(BF16), w_ref[...], preferred_element_type=F32))
    o_ref[0] = (g * gate * _silu(z_ref[0].astype(F32))).astype(o_ref.dtype)


def s5_post(y, proj, d_skip, w_glu_bf16):
    bsz, seq, _ = y.shape
    tm = min(POST_TM, seq)
    return pl.pallas_call(
        _s5_post_body,
        grid=(bsz, seq // tm),
        in_specs=[
            pl.BlockSpec((1, tm, D_INNER), lambda b, s: (b, s, 0)),
            pl.BlockSpec((1, tm, D_INNER), lambda b, s: (b, s, 0)),
            pl.BlockSpec((1, tm, D_INNER), lambda b, s: (b, s, 1)),
            pl.BlockSpec((1, D_INNER), lambda b, s: (0, 0)),
            pl.BlockSpec((D_INNER, D_INNER), lambda b, s: (0, 0)),
        ],
        out_specs=pl.BlockSpec((1, tm, D_INNER), lambda b, s: (b, s, 0)),
        out_shape=jax.ShapeDtypeStruct((bsz, seq, D_INNER), BF16),
        compiler_params=_cparams(("parallel", "parallel")),
        name="s5_post",
    )(y, proj, proj, d_skip.reshape(1, D_INNER).astype(F32), w_glu_bf16)


def _gla_body(q_ref, k_ref, v_ref, z_ref, gk_ref, w2_ref, b2_ref, ng_ref, o_ref, st_ref, att_ref, *, ts):
    C, R = GLA_CHUNK, GLA_SUB
    nsub = C // R
    nt = (((1,), (1,)), ((), ()))

    @pl.when(pl.program_id(2) == 0)
    def _():
        st_ref[...] = jnp.zeros_like(st_ref)

    tri3 = _tri3(C)
    row_r = lax.broadcasted_iota(jnp.int32, (R, C), 0)
    col_r = lax.broadcasted_iota(jnp.int32, (R, C), 1)

    nchunk = ts // C
    qs_all, ks_all, bs_all, decays = [], [], [], []
    for c in range(nchunk):
        rs = slice(c * C, (c + 1) * C)
        x = jnp.dot(gk_ref[0, rs, :].astype(BF16), w2_ref[...], preferred_element_type=F32) + b2_ref[...]
        g = (jnp.minimum(x, 0.0) - jnp.log(1.0 + jnp.exp(-jnp.abs(x)))) * (1.0 / GLA_TAU)
        bs_all.append(_cumsum_rows(tri3, g))
        qs_all.append(q_ref[0, rs, :].astype(F32) * (GLA_DK ** -0.5))
        ks_all.append(k_ref[0, rs, :].astype(F32))
        decays.append(jnp.max(-jnp.sum(g.reshape(nsub, R, GLA_DK), axis=1)))
    sub_decay = functools.reduce(jnp.maximum, decays)

    def scores_matmul(c, i, cap):
        q, k, b = qs_all[c], ks_all[c], bs_all[c]
        bi = b[i * R:(i + 1) * R, :]
        ref = b[i * R - 1:i * R, :] if i > 0 else jnp.zeros((1, GLA_DK), F32)
        qs = q[i * R:(i + 1) * R, :] * jnp.exp(bi - ref)
        ks = k * jnp.exp(jnp.minimum(ref - b, cap))
        return lax.dot_general(qs.astype(BF16), ks.astype(BF16), nt, preferred_element_type=F32)

    @pl.when(sub_decay <= GLA_SAFE_DECAY)
    def _():
        for c in range(nchunk):
            for i in range(nsub):
                att_ref[c, i * R:(i + 1) * R, :] = jnp.where(col_r <= row_r + i * R,
                                                             scores_matmul(c, i, GLA_SAFE_DECAY), 0.0)

    @pl.when(sub_decay > GLA_SAFE_DECAY)
    def _():
        for c in range(nchunk):
            q, k, b = qs_all[c], ks_all[c], bs_all[c]
            for i in range(nsub):
                bi = b[i * R:(i + 1) * R, :]
                qi = q[i * R:(i + 1) * R, :]
                ki = k[i * R:(i + 1) * R, :]
                dg = jnp.zeros((R, C), F32)
                for s in range(R):
                    w = jnp.exp(jnp.minimum(bi - bi[s:s + 1, :], 0.0))
                    cs = jnp.sum(qi * ki[s:s + 1, :] * w, axis=-1, keepdims=True)
                    dg = jnp.where((col_r == i * R + s) & (row_r >= s), cs, dg)
                if i > 0:
                    dg = dg + jnp.where(col_r < i * R, scores_matmul(c, i, 0.0), 0.0)
                att_ref[c, i * R:(i + 1) * R, :] = dg

    for c in range(nchunk):
        rs = slice(c * C, (c + 1) * C)
        q, k, b = qs_all[c], ks_all[c], bs_all[c]
        v = v_ref[0, rs, :]
        state = st_ref[...]
        o = jnp.dot((q * jnp.exp(b)).astype(BF16), state.astype(BF16), preferred_element_type=F32)
        o = o + jnp.dot(att_ref[c].astype(BF16), v, preferred_element_type=F32)

        b_last = b[C - 1:C, :]
        kd = k * jnp.exp(b_last - b)
        dec = jnp.broadcast_to(jnp.exp(b_last), (C, GLA_DK)).T
        upd = jnp.dot(kd.T.astype(BF16), v, preferred_element_type=F32)
        st_ref[...] = state * jnp.concatenate([dec] * (GLA_DV // GLA_DK), axis=1) + upd

        on = o * lax.rsqrt(jnp.mean(o * o, axis=-1, keepdims=True) + EPS) * ng_ref[...]
        o_ref[0, rs, :] = (on * _silu(z_ref[0, rs, :].astype(F32))).astype(o_ref.dtype)


def gla_mixer(proj, gk_aux, w_gk2, b_gk2, norm_g):
    bsz, seq, _ = proj.shape
    ts = min(GLA_TS, seq)
    w2 = jnp.zeros((LANES, GLA_HEADS * GLA_DK), F32).at[:GLA_RANK].set(w_gk2.astype(F32)).astype(BF16)
    return pl.pallas_call(
        functools.partial(_gla_body, ts=ts),
        grid=(bsz, GLA_HEADS, seq // ts),
        in_specs=[
            pl.BlockSpec((1, ts, GLA_DK), lambda b, h, s: (b, s, h)),
            pl.BlockSpec((1, ts, GLA_DK), lambda b, h, s: (b, s, GLA_HEADS + h)),
            pl.BlockSpec((1, ts, GLA_DV), lambda b, h, s: (b, s, 2 + h)),
            pl.BlockSpec((1, ts, GLA_DV), lambda b, h, s: (b, s, 6 + h)),
            pl.BlockSpec((1, ts, LANES), lambda b, h, s: (b, s, 0)),
            pl.BlockSpec((LANES, GLA_DK), lambda b, h, s: (0, h)),
            pl.BlockSpec((1, GLA_DK), lambda b, h, s: (0, h)),
            pl.BlockSpec((1, GLA_DV), lambda b, h, s: (0, 0)),
        ],
        out_specs=pl.BlockSpec((1, ts, GLA_DV), lambda b, h, s: (b, s, h)),
        out_shape=jax.ShapeDtypeStruct((bsz, seq, D_INNER), BF16),
        scratch_shapes=[pltpu.VMEM((GLA_DK, GLA_DV), F32), pltpu.VMEM((ts // GLA_CHUNK, GLA_CHUNK, GLA_CHUNK), F32)],
        compiler_params=_cparams(("parallel", "parallel", "arbitrary")),
        name="gla_mixer",
    )(proj, proj, proj, proj, gk_aux, w2, b_gk2.reshape(1, -1).astype(F32), norm_g.reshape(1, -1).astype(F32))


def _norm_rope(x, g, cos, sin):
    lane = lax.broadcasted_iota(jnp.int32, (1, LANES), 1)
    first = (lane % DIFF_HALF) < (DIFF_HALF // 2)
    gr = lax.broadcasted_iota(jnp.int32, (LANES, LANES), 0) // DIFF_HALF
    gc = lax.broadcasted_iota(jnp.int32, (LANES, LANES), 1) // DIFF_HALF
    gmat = jnp.where(gr == gc, 1.0 / DIFF_HALF, 0.0).astype(BF16)
    x2 = x * x
    x2h = x2.astype(BF16)
    x2l = (x2 - x2h.astype(F32)).astype(BF16)
    ms = jnp.dot(x2h, gmat, preferred_element_type=F32) + jnp.dot(x2l, gmat, preferred_element_type=F32)
    xn = x * lax.rsqrt(ms + EPS) * g
    partner = jnp.where(first, pltpu.roll(xn, LANES - DIFF_HALF // 2, 1), pltpu.roll(xn, DIFF_HALF // 2, 1))
    return xn * cos + partner * jnp.where(first, -sin, sin)


def _diff_prep_body(x_ref, g_ref, cos_ref, sin_ref, o_ref):
    for c in range(D_INNER // LANES):
        x = x_ref[0, :, c * LANES:(c + 1) * LANES].astype(F32)
        o_ref[0, 0, :, c * LANES:(c + 1) * LANES] = _norm_rope(x, g_ref[0], cos_ref[0], sin_ref[0]).astype(o_ref.dtype)


def diff_prep(proj, gains, cos_t, sin_t):
    bsz, seq, _ = proj.shape
    tm = min(POST_TM, seq)
    return pl.pallas_call(
        _diff_prep_body,
        grid=(2, bsz, seq // tm),
        in_specs=[
            pl.BlockSpec((1, tm, D_INNER), lambda w, b, s: (b, s, w)),
            pl.BlockSpec((1, 1, LANES), lambda w, b, s: (w, 0, 0)),
            pl.BlockSpec((1, tm, LANES), lambda w, b, s: (b, s, 0)),
            pl.BlockSpec((1, tm, LANES), lambda w, b, s: (b, s, 0)),
        ],
        out_specs=pl.BlockSpec((1, 1, tm, D_INNER), lambda w, b, s: (w, b, s, 0)),
        out_shape=jax.ShapeDtypeStruct((2, bsz, seq, D_INNER), BF16),
        compiler_params=_cparams(("parallel", "parallel", "parallel")),
        name="diff_prep",
    )(proj, gains, cos_t, sin_t)


def _diff_epilogue(on, lam, sg, z, scale_out):
    r = on.shape[0] // 2
    o = on[:r] - lam * on[r:]
    o = o * lax.rsqrt(jnp.mean(o * o, axis=-1, keepdims=True) + EPS) * sg * scale_out
    return o * _silu(z.astype(F32))


def _diff_attn_online_body(q_ref, k_ref, v_ref, z_ref, lam_ref, sg_ref, o_ref, *, tq, scale_out):
    qi = pl.program_id(2)
    q = q_ref[0, 0]
    lane = lax.broadcasted_iota(jnp.int32, (tq, LANES), 1)
    zero = jnp.zeros_like(q)
    qq = jnp.concatenate([jnp.where(lane < DIFF_HALF, q, zero), jnp.where(lane >= DIFF_HALF, q, zero)], axis=0)

    def block(j, carry, masked):
        m, l, acc = carry
        k0 = pl.multiple_of(j * tq, tq)
        kb = k_ref[0, 0, pl.ds(k0, tq), :]
        vb = v_ref[0, pl.ds(k0, tq), :]
        s = lax.dot_general(qq, kb, (((1,), (1,)), ((), ())), preferred_element_type=F32)
        if masked:
            r = lax.broadcasted_iota(jnp.int32, (2 * tq, tq), 0)
            r = jnp.where(r >= tq, r - tq, r)
            cidx = lax.broadcasted_iota(jnp.int32, (2 * tq, tq), 1)
            s = jnp.where(cidx <= r, s, -jnp.inf)
        m_new = jnp.maximum(m, jnp.max(s, axis=-1, keepdims=True))
        alpha = jnp.exp2(m - m_new)
        p = jnp.exp2(s - m_new)
        l = alpha * l + jnp.sum(p, axis=-1, keepdims=True)
        acc = alpha * acc + jnp.dot(p.astype(BF16), vb, preferred_element_type=F32)
        return m_new, l, acc

    init = (jnp.full((2 * tq, 1), -jnp.inf, F32), jnp.zeros((2 * tq, 1), F32), jnp.zeros((2 * tq, LANES), F32))
    carry = lax.fori_loop(0, qi, lambda j, cr: block(j, cr, False), init)
    m, l, acc = block(qi, carry, True)
    o_ref[0] = _diff_epilogue(acc / l, lam_ref[...], sg_ref[...], z_ref[0], scale_out).astype(o_ref.dtype)


def _diff_attn_direct_body(qraw_ref, kraw_ref, v_ref, z_ref, g_ref, cos_ref, sin_ref, lam_ref, sg_ref, o_ref,
                           q_ref, k_ref, *, seq, scale_out):
    tq, hq = DIFF_TQ, DIFF_TQ // 2
    def prep(i):
        rows = pl.ds(i * tq, tq)
        cos, sin = cos_ref[0, rows, :], sin_ref[0, rows, :]
        q_ref[0, 0, rows, :] = _norm_rope(qraw_ref[0, rows, :].astype(F32), g_ref[0], cos, sin).astype(BF16)
        k_ref[0, 0, rows, :] = _norm_rope(kraw_ref[0, rows, :].astype(F32), g_ref[1], cos, sin).astype(BF16)

    lane =lax.broadcasted_iota(jnp.int32, (tq, LANES), 1)
    tri = (lax.broadcasted_iota(jnp.int32, (2 * hq, hq), 1)
           <= lax.broadcasted_iota(jnp.int32, (2 * hq, hq), 0) % hq)
    nt = (((1,), (1,)), ((), ()))

    def lanesum(p):
        acc = p[:, :LANES]
        for c in range(1, p.shape[1] // LANES):
            acc = acc + p[:, c * LANES:(c + 1) * LANES]
        return acc

    def qtile(i):
        r0 = i * tq
        q = q_ref[0, 0, pl.ds(r0, tq), :]
        zero = jnp.zeros_like(q)
        q0 = jnp.where(lane < DIFF_HALF, q, zero)
        q1 = jnp.where(lane >= DIFF_HALF, q, zero)
        qq = jnp.concatenate([q0[:hq], q1[:hq], q0[hq:], q1[hq:]], axis=0)

        def offdiag(j, carry):
            acc, l = carry
            c0 = j * tq
            p = jnp.exp2(lax.dot_general(qq, k_ref[0, 0, pl.ds(c0, tq), :], nt, preferred_element_type=F32))
            return (acc + jnp.dot(p.astype(BF16), v_ref[0, pl.ds(c0, tq), :], preferred_element_type=F32),
                    l + lanesum(p))

        carry = (jnp.zeros((2 * tq, LANES), F32), jnp.zeros((2 * tq, LANES), F32))
        for j in range(i):
            carry = offdiag(j, carry)
        acc, l = carry
        r1 = r0 + hq
        k_a, v_a = k_ref[0, 0, pl.ds(r0, hq), :], v_ref[0, pl.ds(r0, hq), :]
        k_b, v_b = k_ref[0, 0, pl.ds(r1, hq), :], v_ref[0, pl.ds(r1, hq), :]
        qq_a, qq_b = qq[:tq], qq[tq:]
        p_aa = jnp.where(tri, jnp.exp2(lax.dot_general(qq_a, k_a, nt, preferred_element_type=F32)), 0.0)
        p_ba = jnp.exp2(lax.dot_general(qq_b, k_a, nt, preferred_element_type=F32))
        p_bb = jnp.where(tri, jnp.exp2(lax.dot_general(qq_b, k_b, nt, preferred_element_type=F32)), 0.0)
        acc_a = acc[:tq] + jnp.dot(p_aa.astype(BF16), v_a, preferred_element_type=F32)
        acc_b = (acc[tq:] + jnp.dot(p_ba.astype(BF16), v_a, preferred_element_type=F32)
                 + jnp.dot(p_bb.astype(BF16), v_b, preferred_element_type=F32))
        l_a = jnp.sum(l[:tq] + lanesum(p_aa), axis=-1, keepdims=True)
        l_b = jnp.sum(l[tq:] + lanesum(p_ba) + lanesum(p_bb), axis=-1, keepdims=True)
        o_ref[0, pl.ds(r0, hq), :] = _diff_epilogue(
            acc_a / l_a, lam_ref[...], sg_ref[...], z_ref[0, pl.ds(r0, hq), :], scale_out).astype(o_ref.dtype)
        o_ref[0, pl.ds(r1, hq), :] = _diff_epilogue(
            acc_b / l_b, lam_ref[...], sg_ref[...], z_ref[0, pl.ds(r1, hq), :], scale_out).astype(o_ref.dtype)

    for i in range(seq // tq):
        prep(i)
    for i in range(seq // tq):
        qtile(i)


def diff_attn(proj, gains, cos_t, sin_t, lam_vec, subln_g, lam_init, score_bound_log2):
    bsz, seq, _ = proj.shape
    koff = D_INNER // LANES
    voff = 2 * D_INNER // LANES
    zoff = 3 * D_INNER // LANES
    sg = subln_g.reshape(1, LANES).astype(F32)
    vec = pl.BlockSpec((1, LANES), lambda *_: (0, 0))

    def online(proj, gains, cos_t, sin_t, lam_vec, sg):
        qk = diff_prep(proj, gains, cos_t, sin_t)
        tq = min(DIFF_TQ, seq)
        return pl.pallas_call(
            functools.partial(_diff_attn_online_body, tq=tq, scale_out=1.0 - lam_init),
            grid=(bsz, DIFF_HEADS, seq // tq),
            in_specs=[
                pl.BlockSpec((1, 1, tq, LANES), lambda b, h, i: (0, b, i, h)),
                pl.BlockSpec((1, 1, seq, LANES), lambda b, h, i: (1, b, 0, h)),
                pl.BlockSpec((1, seq, LANES), lambda b, h, i: (b, 0, voff + h)),
                pl.BlockSpec((1, tq, LANES), lambda b, h, i: (b, i, zoff + h)),
                vec, vec,
            ],
            out_specs=pl.BlockSpec((1, tq, LANES), lambda b, h, i: (b, i, h)),
            out_shape=jax.ShapeDtypeStruct((bsz, seq, D_INNER), BF16),
            compiler_params=_cparams(("parallel", "parallel", "arbitrary")),
            name="diff_attn_online",
        )(qk, qk, proj, proj, lam_vec, sg)

    def direct(proj, gains, cos_t, sin_t, lam_vec, sg):
        return pl.pallas_call(
            functools.partial(_diff_attn_direct_body, seq=seq, scale_out=1.0 - lam_init),
            grid=(bsz, DIFF_HEADS),
            in_specs=[
                pl.BlockSpec((1, seq, LANES), lambda b, h: (b, 0, h)),
                pl.BlockSpec((1, seq, LANES), lambda b, h: (b, 0, koff + h)),
                pl.BlockSpec((1, seq, LANES), lambda b, h: (b, 0, voff + h)),
                pl.BlockSpec((1, seq, LANES), lambda b, h: (b, 0, zoff + h)),
                pl.BlockSpec((2, 1, LANES), lambda b, h: (0, 0, 0)),
                pl.BlockSpec((1, seq, LANES), lambda b, h: (b, 0, 0)),
                pl.BlockSpec((1, seq, LANES), lambda b, h: (b, 0, 0)),
                vec, vec,
            ],
            out_specs=pl.BlockSpec((1, seq, LANES), lambda b, h: (b, 0, h)),
            out_shape=jax.ShapeDtypeStruct((bsz, seq, D_INNER), BF16),
            scratch_shapes=[pltpu.VMEM((1, 1, seq, LANES), BF16), pltpu.VMEM((1, 1, seq, LANES), BF16)],
            compiler_params=_cparams(("parallel", "parallel")),
            name="diff_attn_direct",
        )(proj, proj, proj, proj, gains, cos_t, sin_t, lam_vec, sg)

    operands = (proj, gains, cos_t, sin_t, lam_vec, sg)
    if seq % DIFF_TQ != 0:
        return online(*operands)
    return lax.cond(score_bound_log2 <= DIFF_DIRECT_MAX_LOG2, direct, online, *operands)


def _ssd_body(xbc_ref, z_ref, dt_ref, cw_ref, cb_ref, dtb_ref, alog_ref, dsk_ref, ng_ref, e64_ref, e512_ref,
              o_ref, xs_ref, st_ref, *, ts):
    L = SSD_CHUNK
    gw = SSD_HPG * SSD_HEAD_DIM
    pad = 8

    @pl.when(pl.program_id(1) == 0)
    def _():
        st_ref[...] = jnp.zeros_like(st_ref)
        xs_ref[0:pad, :] = jnp.zeros((pad, SSD_CONV_CH), F32)

    @pl.when(pl.program_id(1) > 0)
    def _():
        xs_ref[0:pad, :] = xs_ref[ts:ts + pad, :]

    xs_ref[pad:pad + ts, :] = xbc_ref[0].astype(F32)
    cw = cw_ref[...]
    xt = xs_ref[0:pad + ts, :]
    u = xt * cw[0:1, :]
    for w in range(1, SSD_CONV):
        u = pltpu.roll(u, 1, 0) + xt * cw[w:w + 1, :]
    xc_all = _silu(u[pad:, :] + cb_ref[...])

    tri3 = _tri3(L)
    lane_grp = lax.broadcasted_iota(jnp.int32, (L, LANES), 1) // SSD_HEADS
    row4 = lax.broadcasted_iota(jnp.int32, (L, SSD_HPG * L), 0)
    col4 = lax.broadcasted_iota(jnp.int32, (L, SSD_HPG * L), 1) % L
    causal4 = col4 <= row4
    br = lax.broadcasted_iota(jnp.int32, (SSD_HPG * L, gw), 0) // L
    bc = lax.broadcasted_iota(jnp.int32, (SSD_HPG * L, gw), 1) // SSD_HEAD_DIM
    blockdiag = br == bc
    a_neg = -jnp.exp(alog_ref[...]) * math.log2(math.e)

    for c in range(ts // L):
        rs = slice(c * L, (c + 1) * L)
        xc = xc_all[rs, :]
        xr = dt_ref[0, rs, :] + dtb_ref[...]
        dt = jnp.maximum(xr, 0.0) + jnp.log(1.0 + jnp.exp(-jnp.abs(xr)))
        cum = _cumsum_rows(tri3, dt * a_neg)
        cum_p = _pack_split3(cum, lane_grp)
        cum64 = jnp.dot(cum_p, e64_ref[...], preferred_element_type=F32)
        dt64 = jnp.dot(_pack_split3(dt, lane_grp), e64_ref[...], preferred_element_type=F32)
        cum512 = jnp.dot(cum_p, e512_ref[...], preferred_element_type=F32)
        cum_t = cum.T
        last64 = cum64[L - 1:L, :]
        ys = []
        for g in range(SSD_GROUPS):
            xg = xc[:, g * gw:(g + 1) * gw]
            bg = xc[:, D_INNER + g * SSD_STATE:D_INNER + (g + 1) * SSD_STATE]
            cg = xc[:, D_INNER + SSD_GROUPS * SSD_STATE + g * SSD_STATE:
                    D_INNER + SSD_GROUPS * SSD_STATE + (g + 1) * SSD_STATE]
            cgb = cg.astype(BF16)
            cb = lax.dot_general(cgb, bg.astype(BF16), (((1,), (1,)), ((), ())), preferred_element_type=F32)
            ce = cum512[:, g * SSD_HPG * L:(g + 1) * SSD_HPG * L]
            crow = jnp.concatenate([cum_t[g * SSD_HPG + hh:g * SSD_HPG + hh + 1, :] for hh in range(SSD_HPG)],
                                   axis=1)
            m = jnp.where(causal4, jnp.exp2(jnp.minimum(ce - crow, 0.0)), 0.0) * jnp.concatenate([cb] * SSD_HPG, axis=1)
            cg64 = cum64[:, g * gw:(g + 1) * gw]
            dg64 = dt64[:, g * gw:(g + 1) * gw]
            xdt = xg * dg64
            rhs = jnp.where(blockdiag, jnp.concatenate([xdt] * SSD_HPG, axis=0), 0.0)
            y = jnp.dot(m.astype(BF16), rhs.astype(BF16), preferred_element_type=F32)
            state = st_ref[g]
            y = y + jnp.dot(cgb, state.astype(BF16), preferred_element_type=F32) * jnp.exp2(cg64)
            lg = last64[:, g * gw:(g + 1) * gw]
            xw = xg * (jnp.exp2(lg - cg64) * dg64)
            st_ref[g] = state * jnp.exp2(lg) + jnp.dot(bg.T.astype(BF16), xw.astype(BF16), preferred_element_type=F32)
            ys.append(y + dsk_ref[:, g * gw:(g + 1) * gw] * xg)
        y = jnp.concatenate(ys, axis=1) * _silu(z_ref[0, rs, :].astype(F32))
        y = y * lax.rsqrt(jnp.mean(y * y, axis=-1, keepdims=True) + EPS) * ng_ref[...]
        o_ref[0, rs, :] = y.astype(o_ref.dtype)


def _ssd_expanders():
    e64 = np.zeros((LANES, D_INNER), np.float32)
    e512 = np.zeros((LANES, SSD_HEADS * SSD_CHUNK), np.float32)
    for rep in range(SSD_REPS):
        for h in range(SSD_HEADS):
            e64[rep * SSD_HEADS + h, h * SSD_HEAD_DIM:(h + 1) * SSD_HEAD_DIM] = 1.0
            e512[rep * SSD_HEADS + h, h * SSD_CHUNK:(h + 1) * SSD_CHUNK] = 1.0
    return jnp.asarray(e64, BF16), jnp.asarray(e512, BF16)


def ssd_mixer(proj, dt_aux, conv_w, conv_b, dt_bias, a_log, d_skip, norm_g):
    bsz, seq, _ = proj.shape
    ts = min(SSD_TS, seq)
    e64, e512 = _ssd_expanders()

    def pad128(v):
        return jnp.zeros((1, LANES), F32).at[0, :SSD_REPS * SSD_HEADS].set(jnp.tile(v.astype(F32), SSD_REPS))

    dsk = jnp.repeat(d_skip.astype(F32), SSD_HEAD_DIM).reshape(1, D_INNER)
    full = lambda shape: pl.BlockSpec(shape, lambda b, s: (0,) * len(shape))
    return pl.pallas_call(
        functools.partial(_ssd_body, ts=ts),
        grid=(bsz, seq // ts),
        in_specs=[
            pl.BlockSpec((1, ts, SSD_CONV_CH), lambda b, s: (b, s, 0)),
            pl.BlockSpec((1, ts, D_INNER), lambda b, s: (b, s, 2)),
            pl.BlockSpec((1, ts, LANES), lambda b, s: (b, s, 0)),
            full((SSD_CONV, SSD_CONV_CH)),
            full((1, SSD_CONV_CH)),
            full((1, LANES)),
            full((1, LANES)),
            full((1, D_INNER)),
            full((1, D_INNER)),
            full((LANES, D_INNER)),
            full((LANES, SSD_HEADS * SSD_CHUNK)),
        ],
        out_specs=pl.BlockSpec((1, ts, D_INNER), lambda b, s: (b, s, 0)),
        out_shape=jax.ShapeDtypeStruct((bsz, seq, D_INNER), BF16),
        scratch_shapes=[pltpu.VMEM((ts + 16, SSD_CONV_CH), F32),
                        pltpu.VMEM((SSD_GROUPS, SSD_STATE, SSD_HPG * SSD_HEAD_DIM), F32)],
        compiler_params=_cparams(("parallel", "arbitrary")),
        name="ssd_mixer",
    )(proj, proj, dt_aux, conv_w.astype(F32), conv_b.reshape(1, -1).astype(F32), pad128(dt_bias), pad128(a_log),
      dsk, norm_g.reshape(1, -1).astype(F32), e64, e512)


def _pad_cols(w, width=LANES):
    return jnp.zeros((w.shape[0], width), F32).at[:, :w.shape[1]].set(w.astype(F32))


def s5_layer(x, norm_g, w_in, lam_re, lam_im, log_step, b_re, b_im, c_re, c_im, d_skip, w_glu):
    bsz, seq, d = x.shape
    proj = in_proj(x.reshape(bsz * seq, d), norm_g, w_in.astype(BF16)).reshape(bsz, seq, -1)
    y = s5_slab(proj, *s5_slab_params(lam_re, lam_im, log_step, b_re, b_im, c_re, c_im))
    branch = s5_post(y, proj, d_skip, w_glu.astype(BF16))
    return branch, proj, (2 * D_INNER) // D_X, (2 * D_INNER) // D_X + 1


def gla_layer(x, norm_g, w_in, w_gk2, b_gk2, gla_norm_g):
    bsz, seq, d = x.shape
    nqk = GLA_HEADS * GLA_DK
    c0 = 2 * nqk + D_INNER
    w_main = jnp.concatenate([w_in[:, :c0], w_in[:, c0 + GLA_RANK:]], axis=1).astype(BF16)
    proj, gk_aux = in_proj(x.reshape(bsz * seq, d), norm_g, w_main, _pad_cols(w_in[:, c0:c0 + GLA_RANK]))
    proj = proj.reshape(bsz, seq, -1)
    branch = gla_mixer(proj, gk_aux.reshape(bsz, seq, LANES), w_gk2, b_gk2, gla_norm_g)
    zx_col = (c0 + D_INNER) // D_X
    return branch, proj, zx_col, zx_col + 1


def diff_layer(x, norm_g, w_in, cos_t, sin_t, q_g, k_g, lq1, lk1, lq2, lk2, subln_g, lam_init):
    bsz, seq, d = x.shape
    proj = in_proj(x.reshape(bsz * seq, d), norm_g, w_in.astype(BF16)).reshape(bsz, seq, -1)
    q_scale = (DIFF_HALF ** -0.5) * math.log2(math.e)
    gains = jnp.stack([jnp.tile(q_g.astype(F32), LANES // DIFF_HALF) * q_scale,
                       jnp.tile(k_g.astype(F32), LANES // DIFF_HALF)]).reshape(2, 1, LANES)
    score_bound_log2 = (BF16_ROUNDING_SLACK * DIFF_HALF * q_scale
                        * jnp.max(jnp.abs(q_g.astype(F32))) * jnp.max(jnp.abs(k_g.astype(F32))))
    lam = (jnp.exp(jnp.sum(lq1.astype(F32) * lk1.astype(F32)))
           - jnp.exp(jnp.sum(lq2.astype(F32) * lk2.astype(F32))) + lam_init)
    lam_vec = jnp.full((1, LANES), lam, F32)
    branch = diff_attn(proj, gains, cos_t, sin_t, lam_vec, subln_g, lam_init, score_bound_log2)
    zx_col = (4 * D_INNER) // D_X
    return branch, proj, zx_col, zx_col + 1


def ssd_layer(x, norm_g, w_in, conv_w, conv_b, dt_bias, a_log, d_skip, ssd_norm_g):
    bsz, seq, d = x.shape
    c0 = SSD_CONV_CH
    w_main = jnp.concatenate([w_in[:, :c0], w_in[:, c0 + SSD_HEADS:]], axis=1).astype(BF16)
    w_dt = jnp.tile(w_in[:, c0:c0 + SSD_HEADS], (1, SSD_REPS))
    proj, dt_aux = in_proj(x.reshape(bsz * seq, d), norm_g, w_main, _pad_cols(w_dt))
    proj = proj.reshape(bsz, seq, -1)
    branch = ssd_mixer(proj, dt_aux.reshape(bsz, seq, LANES), conv_w, conv_b, dt_bias, a_log, d_skip, ssd_norm_g)
    zx_col = (c0 + D_INNER) // D_X
    return branch, proj, zx_col, zx_col + 1


def rope_tables(positions):
    inv = ROPE_THETA ** (-jnp.arange(0, DIFF_HALF, 2, dtype=F32) / DIFF_HALF)
    ang = positions.astype(F32)[..., None] * inv
    reps = LANES // (DIFF_HALF // 2)
    return jnp.tile(jnp.cos(ang), (1, 1, reps)), jnp.tile(jnp.sin(ang), (1, 1, reps))


def kernel(x, mem, positions, norm_g, w_out, mem_norm_g, w_mem_kv, xq_g, xk_g, s5_w_in, s5_lam_re, s5_lam_im, s5_log_step, s5_b_re, s5_b_im, s5_c_re, s5_c_im, s5_d, s5_w_glu, gla_w_in, gla_w_gk2, gla_b_gk2, gla_norm_g, diff_w_in, diff_q_g, diff_k_g, diff_lq1, diff_lk1, diff_lq2, diff_lk2, diff_subln_g, ssd_w_in, ssd_conv_w, ssd_conv_b, ssd_dt_bias, ssd_a_log, ssd_d, ssd_norm_g):
    depth = norm_g.shape[0]
    k_all, v_all = mem_kv(mem, mem_norm_g, w_mem_kv, xk_g)
    cos_t, sin_t = rope_tables(positions)
    w_out_b = w_out.astype(BF16)
    for i in range(depth):
        kind, j = i % 4, i // 4
        if kind == 0:
            branch, proj, zc, qc = s5_layer(x, norm_g[i], s5_w_in[j], s5_lam_re[j], s5_lam_im[j], s5_log_step[j],
                                            s5_b_re[j], s5_b_im[j], s5_c_re[j], s5_c_im[j], s5_d[j], s5_w_glu[j])
        elif kind == 1:
            branch, proj, zc, qc = gla_layer(x, norm_g[i], gla_w_in[j], gla_w_gk2[j], gla_b_gk2[j], gla_norm_g[j])
        elif kind == 2:
            lam_init = 0.8 - 0.6 * math.exp(-0.3 * i)
            branch, proj, zc, qc = diff_layer(x, norm_g[i], diff_w_in[j], cos_t, sin_t, diff_q_g[j], diff_k_g[j],
                                              diff_lq1[j], diff_lk1[j], diff_lq2[j], diff_lk2[j], diff_subln_g[j],
                                              lam_init)
        else:
            branch, proj, zc, qc = ssd_layer(x, norm_g[i], ssd_w_in[j], ssd_conv_w[j], ssd_conv_b[j], ssd_dt_bias[j],
                                             ssd_a_log[j], ssd_d[j], ssd_norm_g[j])
        x = out_stage(branch, proj, zc, qc, k_all[i], v_all[i], x, w_out_b[i], xq_g[i])
    return x
```
